```python
import jax, jax.numpy as jnp
from jax import lax
import numpy as np

D_MODEL = 1024
BATCH = 16
SEQ = 2048
DEPTH = 4

N_MIXERS = 2
N_MLA = (DEPTH + 1) // 2
N_HGRN = DEPTH // 2

MLA_HEADS = 16
QK_NOPE = 64
QK_ROPE = 32
V_HEAD = 64
Q_LORA = 768
KV_LORA = 256
ROPE_THETA = 10000.0
Q_BLOCK = 128

HGRN_EXPAND = 128
HGRN_HEADS = D_MODEL // HGRN_EXPAND
HGRN_V = D_MODEL // HGRN_HEADS
HGRN_CHUNK = 64

D_FF = -(-8 * D_MODEL // (3 * 256)) * 256

ALPHA = (2.0 * DEPTH) ** 0.25
BETA = (8.0 * DEPTH) ** -0.25
LN_EPS = 1e-5
RMS_EPS = 1e-6

kernel_name = 'hybrid_mla_hgrn2_deepnorm_adaln'


def layer_norm(x, g, b):
    xf = x.astype(jnp.float32)
    mu = jnp.mean(xf, -1, keepdims=True)
    var = jnp.mean(jnp.square(xf - mu), -1, keepdims=True)
    return ((xf - mu) * lax.rsqrt(var + LN_EPS) * g + b).astype(x.dtype)


def rms_norm(x, g):
    xf = x.astype(jnp.float32)
    ms = jnp.mean(jnp.square(xf), -1, keepdims=True)
    return (xf * lax.rsqrt(ms + RMS_EPS) * g).astype(x.dtype)


def rope_cos_sin(positions):
    inv_freq = ROPE_THETA ** (-jnp.arange(0, QK_ROPE, 2, dtype=jnp.float32) / QK_ROPE)
    ang = positions.astype(jnp.float32)[..., None] * inv_freq
    return jnp.cos(ang), jnp.sin(ang)


def apply_rope(x, cos, sin):
    x1, x2 = jnp.split(x.astype(jnp.float32), 2, axis=-1)
    return jnp.concatenate([x1 * cos - x2 * sin, x1 * sin + x2 * cos], -1).astype(x.dtype)


def causal_mla_attention(q_nope, q_rope, k_nope, k_rope, v):
    S = q_nope.shape[1]
    scale = (QK_NOPE + QK_ROPE) ** -0.5
    neg = jnp.finfo(jnp.float32).min
    outs = []
    for blk in range(S // Q_BLOCK):
        q0 = blk * Q_BLOCK
        kend = q0 + Q_BLOCK
        s = (jnp.einsum('bqhd,bkhd->bhqk', q_nope[:, q0:kend], k_nope[:, :kend])
             + jnp.einsum('bqhr,bkr->bhqk', q_rope[:, q0:kend], k_rope[:, :kend]))
        s = s.astype(jnp.float32) * scale
        mask = (q0 + jnp.arange(Q_BLOCK))[:, None] >= jnp.arange(kend)[None, :]
        p = jax.nn.softmax(jnp.where(mask, s, neg), axis=-1).astype(v.dtype)
        outs.append(jnp.einsum('bhqk,bkhd->bqhd', p, v[:, :kend]))
    return jnp.concatenate(outs, axis=1)


def mla(h, cos, sin, w_in, q_norm_g, w_qb, kv_norm_g, w_kvb, w_o):
    B, S, _ = h.shape
    proj = h @ w_in
    q_lat, kv_lat, k_rope = jnp.split(proj, [Q_LORA, Q_LORA + KV_LORA], axis=-1)
    q = (rms_norm(q_lat, q_norm_g) @ w_qb).reshape(B, S, MLA_HEADS, QK_NOPE + QK_ROPE)
    kv = (rms_norm(kv_lat, kv_norm_g) @ w_kvb).reshape(B, S, MLA_HEADS, QK_NOPE + V_HEAD)
    q_nope, q_rope = jnp.split(q, [QK_NOPE], axis=-1)
    k_nope, v = jnp.split(kv, [QK_NOPE], axis=-1)
    q_rope = apply_rope(q_rope, cos[:, :, None, :], sin[:, :, None, :])
    k_rope = apply_rope(k_rope, cos, sin)
    o = causal_mla_attention(q_nope, q_rope, k_nope, k_rope, v)
    return o.reshape(B, S, MLA_HEADS * V_HEAD) @ w_o


def chunk_gated_recurrence(q, k, v, log_f):
    B, S, H, K = q.shape
    V = v.shape[-1]
    C = HGRN_CHUNK
    N = S // C

    def to_chunks(t):
        return t.reshape(B, N, C, H, t.shape[-1]).transpose(1, 0, 3, 2, 4)

    causal = jnp.tril(jnp.ones((C, C), dtype=bool))[:, :, None]

    def step(state, inp):
        q_c, k_c, v_c, g_c = inp
        b = jnp.cumsum(g_c, axis=-2)
        diff = b[..., :, None, :] - b[..., None, :, :]
        decay = jnp.where(causal, jnp.exp(jnp.where(causal, diff, 0.0)), 0.0)
        attn = jnp.einsum('bhtk,bhsk,bhtsk->bhts', q_c, k_c, decay)
        o = (jnp.einsum('bhts,bhsv->bhtv', attn, v_c)
             + jnp.einsum('bhtk,bhkv->bhtv', q_c * jnp.exp(b), state))
        b_last = b[..., -1:, :]
        state = (jnp.exp(b_last[..., 0, :])[..., None] * state
                 + jnp.einsum('bhsk,bhsv->bhkv', k_c * jnp.exp(b_last - b), v_c))
        return state, o

    state0 = jnp.zeros((B, H, K, V), jnp.float32)
    _, o = lax.scan(step, state0, (to_chunks(q), to_chunks(k), to_chunks(v), to_chunks(log_f)))
    return o.transpose(1, 0, 3, 2, 4).reshape(B, S, H, V)


def hgrn2(h, lb, w_in, g_norm_g, w_o):
    B, S, _ = h.shape
    HK = HGRN_HEADS * HGRN_EXPAND
    HV = HGRN_HEADS * HGRN_V
    q, fx, i, g = jnp.split(h @ w_in, [HK, 2 * HK, 2 * HK + HV], axis=-1)
    q = jax.nn.silu(q.astype(jnp.float32)).reshape(B, S, HGRN_HEADS, HGRN_EXPAND)
    fx = fx.astype(jnp.float32).reshape(B, S, HGRN_HEADS, HGRN_EXPAND)
    lb = lb.astype(jnp.float32).reshape(HGRN_HEADS, HGRN_EXPAND)
    sig = jax.nn.sigmoid(fx)
    f = lb + (1.0 - lb) * sig
    log_f = jnp.log(f)
    k = 1.0 - f
    v = i.astype(jnp.float32).reshape(B, S, HGRN_HEADS, HGRN_V)
    o = chunk_gated_recurrence(q, k, v, log_f)
    o = rms_norm(o, g_norm_g).reshape(B, S, HV).astype(h.dtype)
    return (o * jax.nn.silu(g)) @ w_o


def swiglu(h, w_in, w_out):
    gate, up = jnp.split(h @ w_in, 2, axis=-1)
    return (jax.nn.silu(gate) * up) @ w_out


def ada_mod(c, w, b):
    mod = (jax.nn.silu(c) @ w + b)[:, None, :]
    shift, scale, gate = jnp.split(mod, 3, axis=-1)
    return shift, scale, gate


def _w(k, shape, fan_in, scale=1.0):
    return jax.random.normal(k, shape, jnp.float32) * (scale * fan_in ** -0.5)


def setup_inputs(seed: int = 0) -> dict:
    key = jax.random.key(seed)
    ks = jax.random.split(key, 24)
    D = D_MODEL
    x = jax.random.normal(ks[0], (BATCH, SEQ, D), jnp.float32)
    c = jax.random.normal(ks[1], (BATCH, D), jnp.float32)
    offsets = jax.random.randint(ks[2], (BATCH, 1), 0, 4096, dtype=jnp.int32)
    positions = offsets + jnp.arange(SEQ, dtype=jnp.int32)[None, :]
    mla_w_in = _w(ks[3], (N_MLA, D, Q_LORA + KV_LORA + QK_ROPE), D)
    mla_q_norm = 1.0 + 0.02 * jax.random.normal(ks[4], (N_MLA, Q_LORA), jnp.float32)
    mla_w_qb = _w(ks[5], (N_MLA, Q_LORA, MLA_HEADS * (QK_NOPE + QK_ROPE)), Q_LORA)
    mla_kv_norm = 1.0 + 0.02 * jax.random.normal(ks[6], (N_MLA, KV_LORA), jnp.float32)
    mla_w_kvb = _w(ks[7], (N_MLA, KV_LORA, MLA_HEADS * (QK_NOPE + V_HEAD)), KV_LORA)
    mla_w_o = _w(ks[8], (N_MLA, MLA_HEADS * V_HEAD, D), MLA_HEADS * V_HEAD, BETA)
    hgrn_lb = 0.5 * jax.random.normal(ks[9], (N_HGRN, HGRN_HEADS * HGRN_EXPAND), jnp.float32)
    hgrn_w_in = _w(ks[10], (N_HGRN, D, 2 * HGRN_HEADS * HGRN_EXPAND + HGRN_HEADS * HGRN_V + D), D)
    hgrn_g_norm = 1.0 + 0.02 * jax.random.normal(ks[11], (N_HGRN, HGRN_V), jnp.float32)
    hgrn_w_o = _w(ks[12], (N_HGRN, HGRN_HEADS * HGRN_V, D), HGRN_HEADS * HGRN_V, BETA)
    ffn_w_in = _w(ks[13], (DEPTH, D, 2 * D_FF), D)
    ffn_w_out = _w(ks[14], (DEPTH, D_FF, D), D_FF, BETA)
    ada_w = _w(ks[15], (DEPTH, 2, D, 3 * D), D, 0.1)
    ada_b = 0.01 * jax.random.normal(ks[16], (DEPTH, 2, 3 * D), jnp.float32)
    ln_g = 1.0 + 0.02 * jax.random.normal(ks[17], (DEPTH, 2, D), jnp.float32)
    ln_b = 0.01 * jax.random.normal(ks[18], (DEPTH, 2, D), jnp.float32)
    return {'x': x, 'c': c, 'positions': positions,
            'mla_w_in': mla_w_in, 'mla_q_norm': mla_q_norm, 'mla_w_qb': mla_w_qb,
            'mla_kv_norm': mla_kv_norm, 'mla_w_kvb': mla_w_kvb, 'mla_w_o': mla_w_o,
            'hgrn_lb': hgrn_lb, 'hgrn_w_in': hgrn_w_in, 'hgrn_g_norm': hgrn_g_norm, 'hgrn_w_o': hgrn_w_o,
            'ffn_w_in': ffn_w_in, 'ffn_w_out': ffn_w_out,
            'ada_w': ada_w, 'ada_b': ada_b, 'ln_g': ln_g, 'ln_b': ln_b}


def reference(x, c, positions, mla_w_in, mla_q_norm, mla_w_qb, mla_kv_norm, mla_w_kvb, mla_w_o,
              hgrn_lb, hgrn_w_in, hgrn_g_norm, hgrn_w_o, ffn_w_in, ffn_w_out,
              ada_w, ada_b, ln_g, ln_b):
    cos, sin = rope_cos_sin(positions)
    lb_soft = jax.nn.softmax(hgrn_lb.astype(jnp.float32), axis=0)
    lower_bounds = jnp.cumsum(lb_soft, axis=0) - lb_soft[0]
    for layer in range(DEPTH):
        j = layer // N_MIXERS
        shift, scale, gate = ada_mod(c, ada_w[layer, 0], ada_b[layer, 0])
        h = x * (1.0 + scale) + shift
        if layer % N_MIXERS == 0:
            y = mla(h, cos, sin, mla_w_in[j], mla_q_norm[j], mla_w_qb[j],
                    mla_kv_norm[j], mla_w_kvb[j], mla_w_o[j])
        else:
            y = hgrn2(h, lower_bounds[j], hgrn_w_in[j], hgrn_g_norm[j], hgrn_w_o[j])
        x = layer_norm(ALPHA * x + (1.0 + gate) * y, ln_g[layer, 0], ln_b[layer, 0])
        shift, scale, gate = ada_mod(c, ada_w[layer, 1], ada_b[layer, 1])
        h = x * (1.0 + scale) + shift
        y = swiglu(h, ffn_w_in[layer], ffn_w_out[layer])
        x = layer_norm(ALPHA * x + (1.0 + gate) * y, ln_g[layer, 1], ln_b[layer, 1])
    return x
```

```python
import functools

import jax
import jax.numpy as jnp
from jax import lax
from jax.experimental import pallas as pl
from jax.experimental.pallas import tpu as pltpu

F32 = jnp.float32
BF16 = jnp.bfloat16

DEPTH = 4
N_MIXERS = 2

MLA_HEADS = 16
QK_NOPE = 64
QK_ROPE = 32
V_HEAD = 64
Q_LORA = 768
KV_LORA = 256
ROPE_THETA = 10000.0
HEAD_PAD = 128
ROPE_LO = QK_NOPE
ROPE_HALF = QK_ROPE // 2

HGRN_EXPAND = 128
HGRN_CHUNK = 64
HGRN_SUB = 16

ALPHA = (2.0 * DEPTH) ** 0.25
LN_EPS = 1e-5
RMS_EPS = 1e-6

VMEM_LIMIT_BYTES = 56 * 1024 * 1024


def _params(*semantics):
    return pltpu.CompilerParams(dimension_semantics=semantics,
                                vmem_limit_bytes=VMEM_LIMIT_BYTES)


def _dot(a, b):
    return jnp.dot(a, b, preferred_element_type=F32)


def _dot_nt(a, b):
    return lax.dot_general(a, b, (((1,), (1,)), ((), ())), preferred_element_type=F32)


def _dot_tn(a, b):
    return lax.dot_general(a, b, (((0,), (0,)), ((), ())), preferred_element_type=F32)


def _silu(x):
    return x * jax.nn.sigmoid(x)


def _modulate(x, mod):
    return x * (1.0 + mod[1:2]) + mod[0:1]


def _deepnorm_ln(x, y, mod, g, b):
    z = ALPHA * x + (1.0 + mod[2:3]) * y
    mu = jnp.mean(z, axis=-1, keepdims=True)
    zc = z - mu
    var = jnp.mean(zc * zc, axis=-1, keepdims=True)
    return zc * lax.rsqrt(var + LN_EPS) * g + b


def _rms(x, g):
    ms = jnp.mean(x * x, axis=-1, keepdims=True)
    return x * lax.rsqrt(ms + RMS_EPS) * g


def _ada_kernel(c_ref, w_ref, b_ref, o_ref):
    sc = _silu(c_ref[...]).astype(BF16)
    o_ref[0] = _dot(sc, w_ref[0].astype(BF16)) + b_ref[0]


def ada_mods(c, ada_w, ada_b):
    B, D = c.shape
    L = ada_w.shape[0] * ada_w.shape[1]
    w = ada_w.reshape(L, D, 3 * D)
    b = ada_b.reshape(L, 1, 3 * D)
    out = pl.pallas_call(
        _ada_kernel,
        grid=(L, 3),
        in_specs=[pl.BlockSpec((B, D), lambda l, j: (0, 0)),
                  pl.BlockSpec((1, D, D), lambda l, j: (l, 0, j)),
                  pl.BlockSpec((1, 1, D), lambda l, j: (l, 0, j))],
        out_specs=pl.BlockSpec((1, B, D), lambda l, j: (l, 0, j)),
        out_shape=jax.ShapeDtypeStruct((L, B, 3 * D), F32),
        compiler_params=_params("arbitrary", "arbitrary"),
        name="ada_mods",
    )(c, w, b)
    return out.reshape(L, B, 3, D)


def _trig_kernel(pos_ref, inv_ref, cos_ref, sin_ref):
    ang = pos_ref[...].astype(F32) * inv_ref[...]
    cos_ref[...] = jnp.cos(ang)
    sin_ref[...] = jnp.sin(ang)


def rope_tables(positions):
    n = positions.size
    lanes = 128
    per_row = lanes // ROPE_HALF
    rows = n // per_row
    pos_rep = jnp.repeat(positions.reshape(-1), ROPE_HALF).reshape(rows, lanes)
    inv_freq = ROPE_THETA ** (-jnp.arange(0, QK_ROPE, 2, dtype=F32) / QK_ROPE)
    inv = jnp.tile(inv_freq, per_row).reshape(1, lanes)
    tr = min(rows, 512)
    cos, sin = pl.pallas_call(
        _trig_kernel,
        grid=(rows // tr,),
        in_specs=[pl.BlockSpec((tr, lanes), lambda i: (i, 0)),
                  pl.BlockSpec((1, lanes), lambda i: (0, 0))],
        out_specs=[pl.BlockSpec((tr, lanes), lambda i: (i, 0))] * 2,
        out_shape=[jax.ShapeDtypeStruct((rows, lanes), F32)] * 2,
        compiler_params=_params("arbitrary"),
        name="rope_trig",
    )(pos_rep, inv)
    cos = cos.reshape(n, ROPE_HALF)
    sin = sin.reshape(n, ROPE_HALF)
    tail = HEAD_PAD - ROPE_LO - QK_ROPE
    ctab = jnp.concatenate([jnp.ones((n, ROPE_LO), F32), cos, cos, jnp.ones((n, tail), F32)], axis=1)
    stab = jnp.concatenate([jnp.zeros((n, ROPE_LO), F32), sin, sin, jnp.zeros((n, tail), F32)], axis=1)
    return ctab, stab


def _rope_slot(x, ctab, stab, lane):
    other = jnp.where(lane < ROPE_LO + ROPE_HALF,
                      -pltpu.roll(x, HEAD_PAD - ROPE_HALF, 1),
                      pltpu.roll(x, ROPE_HALF, 1))
    return x * ctab + other * stab


def _mla_proj_kernel(x_ref, mod_ref, w_in_ref, qg_ref, w_qb_ref, kvg_ref, w_kvb_ref,
                     ctab_ref, stab_ref, q_ref, k_ref, v_ref):
    h = _modulate(x_ref[...], mod_ref[0]).astype(BF16)
    proj = _dot(h, w_in_ref[...])
    qn = _rms(proj[:, :Q_LORA], qg_ref[...]).astype(BF16)
    kvn = _rms(proj[:, Q_LORA:Q_LORA + KV_LORA], kvg_ref[...]).astype(BF16)
    kr = proj[:, Q_LORA + KV_LORA:]
    q = _dot(qn, w_qb_ref[...])
    kv = _dot(kvn, w_kvb_ref[...])
    ctab = ctab_ref[...]
    stab = stab_ref[...]
    lane = lax.broadcasted_iota(jnp.int32, ctab.shape, 1)
    kr = _rope_slot(kr, ctab, stab, lane)
    scale = (QK_NOPE + QK_ROPE) ** -0.5
    for hd in range(MLA_HEADS):
        sl = slice(hd * HEAD_PAD, (hd + 1) * HEAD_PAD)
        q_ref[:, sl] = (_rope_slot(q[:, sl], ctab, stab, lane) * scale).astype(BF16)
        k_ref[:, sl] = (kv[:, sl] + kr).astype(BF16)
    v_ref[...] = kv[:, MLA_HEADS * HEAD_PAD:].astype(BF16)


def mla_proj(x, mod, w_in_p, qg, w_qb_p, kvg, w_kvb_p, ctab, stab, B, S, tm):
    N, D = x.shape
    nt = S // tm
    row = lambda b, i: (b * nt + i, 0)
    full = lambda b, i: (0, 0)
    HP = MLA_HEADS * HEAD_PAD
    HV = MLA_HEADS * V_HEAD
    return pl.pallas_call(
        _mla_proj_kernel,
        grid=(B, nt),
        in_specs=[pl.BlockSpec((tm, D), row),
                  pl.BlockSpec((1, 3, D), lambda b, i: (b, 0, 0)),
                  pl.BlockSpec(w_in_p.shape, full),
                  pl.BlockSpec(qg.shape, full),
                  pl.BlockSpec(w_qb_p.shape, full),
                  pl.BlockSpec(kvg.shape, full),
                  pl.BlockSpec(w_kvb_p.shape, full),
                  pl.BlockSpec((tm, HEAD_PAD), row),
                  pl.BlockSpec((tm, HEAD_PAD), row)],
        out_specs=[pl.BlockSpec((tm, HP), row),
                   pl.BlockSpec((tm, HP), row),
                   pl.BlockSpec((tm, HV), row)],
        out_shape=[jax.ShapeDtypeStruct((N, HP), BF16),
                   jax.ShapeDtypeStruct((N, HP), BF16),
                   jax.ShapeDtypeStruct((N, HV), BF16)],
        compiler_params=_params("arbitrary", "arbitrary"),
        name="mla_proj",
    )(x, mod, w_in_p, qg, w_qb_p, kvg, w_kvb_p, ctab, stab)


def _attn_kernel(q_ref, k_ref, v_ref, o_ref, *, tq):
    qi = pl.program_id(1)
    tk = tq
    row = lax.broadcasted_iota(jnp.int32, (tq, tk), 0)
    col = lax.broadcasted_iota(jnp.int32, (tq, tk), 1)
    causal = row >= col
    lane = lax.broadcasted_iota(jnp.int32, (tq, 2 * V_HEAD), 1)

    def one_head(hd):
        q = q_ref[:, hd * HEAD_PAD:(hd + 1) * HEAD_PAD]
        vsl = slice((hd // 2) * 2 * V_HEAD, (hd // 2 + 1) * 2 * V_HEAD)

        def step(j, carry, masked):
            m, l, acc = carry
            r0 = pl.multiple_of(j * tk, tk)
            k = k_ref[pl.ds(r0, tk), hd * HEAD_PAD:(hd + 1) * HEAD_PAD]
            v = v_ref[pl.ds(r0, tk), vsl]
            s = _dot_nt(q, k)
            if masked:
                s = jnp.where(causal, s, -1e30)
            m_new = jnp.maximum(m, jnp.max(s, axis=-1, keepdims=True))
            a = jnp.exp(m - m_new)
            p = jnp.exp(s - m_new)
            l = a * l + jnp.sum(p, axis=-1, keepdims=True)
            acc = a * acc + _dot(p.astype(BF16), v)
            return m_new, l, acc

        init = (jnp.full((tq, 1), -1e30, F32), jnp.zeros((tq, 1), F32),
                jnp.zeros((tq, 2 * V_HEAD), F32))
        carry = lax.fori_loop(0, qi, lambda j, c: step(j, c, False), init)
        _, l, acc = step(qi, carry, True)
        return acc / l

    for hp in range(MLA_HEADS // 2):
        o = jnp.where(lane < V_HEAD, one_head(2 * hp), one_head(2 * hp + 1))
        o_ref[:, hp * 2 * V_HEAD:(hp + 1) * 2 * V_HEAD] = o.astype(BF16)


def mla_attention(q, k, v, B, S, tq):
    N = q.shape[0]
    nt = S // tq
    HP = MLA_HEADS * HEAD_PAD
    HV = MLA_HEADS * V_HEAD
    row = lambda b, i: (b * nt + i, 0)
    seq = lambda b, i: (b, 0)
    return pl.pallas_call(
        functools.partial(_attn_kernel, tq=tq),
        grid=(B, nt),
        in_specs=[pl.BlockSpec((tq, HP), row),
                  pl.BlockSpec((S, HP), seq),
                  pl.BlockSpec((S, HV), seq)],
        out_specs=pl.BlockSpec((tq, HV), row),
        out_shape=jax.ShapeDtypeStruct((N, HV), BF16),
        compiler_params=_params("arbitrary", "arbitrary"),
        name="mla_attention",
    )(q, k, v)


def _proj_ln_kernel(o_ref, x_ref, mod_ref, w_ref, g_ref, b_ref, out_ref):
    y = _dot(o_ref[...], w_ref[...])
    out_ref[...] = _deepnorm_ln(x_ref[...], y, mod_ref[0], g_ref[...], b_ref[...])


def proj_ln(o, x, mod, w, g, b, B, S, tm):
    N, D = x.shape
    nt = S // tm
    row = lambda bb, i: (bb * nt + i, 0)
    full = lambda bb, i: (0, 0)
    return pl.pallas_call(
        _proj_ln_kernel,
        grid=(B, nt),
        in_specs=[pl.BlockSpec((tm, o.shape[1]), row),
                  pl.BlockSpec((tm, D), row),
                  pl.BlockSpec((1, 3, D), lambda bb, i: (bb, 0, 0)),
                  pl.BlockSpec(w.shape, full),
                  pl.BlockSpec((1, D), full),
                  pl.BlockSpec((1, D), full)],
        out_specs=pl.BlockSpec((tm, D), row),
        out_shape=jax.ShapeDtypeStruct((N, D), F32),
        compiler_params=_params("arbitrary", "arbitrary"),
        name="proj_ln",
    )(o, x, mod, w, g, b)


def _ffn_kernel(x_ref, mod_ref, w_in_ref, w_out_ref, g_ref, b_ref, out_ref, *, d_ff):
    x = x_ref[...]
    mod = mod_ref[0]
    h = _modulate(x, mod).astype(BF16)
    gate = _dot(h, w_in_ref[:, :d_ff])
    up = _dot(h, w_in_ref[:, d_ff:])
    act = (_silu(gate) * up).astype(BF16)
    y = _dot(act, w_out_ref[...])
    out_ref[...] = _deepnorm_ln(x, y, mod, g_ref[...], b_ref[...])


def ffn(x, mod, w_in, w_out, g, b, B, S, tm):
    N, D = x.shape
    d_ff = w_out.shape[0]
    nt = S // tm
    row = lambda bb, i: (bb * nt + i, 0)
    full = lambda bb, i: (0, 0)
    return pl.pallas_call(
        functools.partial(_ffn_kernel, d_ff=d_ff),
        grid=(B, nt),
        in_specs=[pl.BlockSpec((tm, D), row),
                  pl.BlockSpec((1, 3, D), lambda bb, i: (bb, 0, 0)),
                  pl.BlockSpec(w_in.shape, full, pipeline_mode=pl.Buffered(1)),
                  pl.BlockSpec(w_out.shape, full, pipeline_mode=pl.Buffered(1)),
                  pl.BlockSpec((1, D), full),
                  pl.BlockSpec((1, D), full)],
        out_specs=pl.BlockSpec((tm, D), row),
        out_shape=jax.ShapeDtypeStruct((N, D), F32),
        compiler_params=_params("arbitrary", "arbitrary"),
        name="ffn",
    )(x, mod, w_in, w_out, g, b)


def _split3(x):
    a = x.astype(BF16)
    r = x - a.astype(F32)
    b = r.astype(BF16)
    c = (r - b.astype(F32)).astype(BF16)
    return a, b, c


def _hgrn_kernel(x_ref, mod_ref, lb_ref, w_in_ref, gn_ref, w_o_ref, g_ref, b_ref, out_ref,
                 q_s, k_s, b_s, v_s, o_s, st_s, *, layer_idx, heads, tm):
    C = HGRN_CHUNK
    SUB = HGRN_SUB
    K = HGRN_EXPAND
    HK = heads * K
    n_chunks = tm // C

    @pl.when(pl.program_id(1) == 0)
    def _():
        st_s[...] = jnp.zeros(st_s.shape, F32)

    x = x_ref[...]
    mod = mod_ref[0]
    h = _modulate(x, mod).astype(BF16)

    lb_all = lb_ref[...]
    lb_max = jnp.max(lb_all, axis=0, keepdims=True)
    lb_exp = jnp.exp(lb_all - lb_max)
    lb_soft = lb_exp / jnp.sum(lb_exp, axis=0, keepdims=True)
    lb = jnp.sum(lb_soft[:layer_idx + 1], axis=0, keepdims=True) - lb_soft[0:1]

    q_all = _silu(_dot(h, w_in_ref[:, :HK]))
    f = lb + (1.0 - lb) * jax.nn.sigmoid(_dot(h, w_in_ref[:, HK:2 * HK]))
    logf = jnp.log(f)
    k_all = 1.0 - f
    v_all = _dot(h, w_in_ref[:, 2 * HK:3 * HK])

    tri = (lax.broadcasted_iota(jnp.int32, (C, C), 0)
           >= lax.broadcasted_iota(jnp.int32, (C, C), 1)).astype(BF16)
    for c in range(n_chunks):
        rows = slice(c * C, (c + 1) * C)
        g1, g2, g3 = _split3(logf[rows])
        bc = _dot(tri, g1) + _dot(tri, g2) + _dot(tri, g3)
        for hd in range(heads):
            b_s[hd, rows, :] = bc[:, hd * K:(hd + 1) * K]
    for hd in range(heads):
        sl = slice(hd * K, (hd + 1) * K)
        q_s[hd] = q_all[:, sl]
        k_s[hd] = k_all[:, sl]
        v_s[hd] = v_all[:, sl]

    sub_row = lax.broadcasted_iota(jnp.int32, (SUB, K), 0)

    def chunk_head(idx, _):
        c = idx // heads
        hd = idx % heads
        r0 = pl.multiple_of(c * C, C)
        q = q_s[hd, pl.ds(r0, C), :]
        k = k_s[hd, pl.ds(r0, C), :]
        b = b_s[hd, pl.ds(r0, C), :]
        v = v_s[hd, pl.ds(r0, C), :]
        v16 = v.astype(BF16)
        st = st_s[hd]
        b_last = b[C - 1:C]

        o_inter = _dot_nt((q * jnp.exp(b)).astype(BF16), st.astype(BF16))
        k_dec = (k * jnp.exp(b_last - b)).astype(BF16)
        st_s[hd] = st * jnp.exp(b_last) + _dot_tn(v16, k_dec)

        outs = []
        for i in range(C // SUB):
            lo = i * SUB
            bi = b[lo:lo + SUB]
            qi = q[lo:lo + SUB]
            ki = k[lo:lo + SUB]
            vi = v[lo:lo + SUB]
            o_i = o_inter[lo:lo + SUB]
            if i > 0:
                ref = b[lo - 1:lo]
                q_t = (qi * jnp.exp(bi - ref)).astype(BF16)
                k_t = (k[:lo] * jnp.exp(ref - b[:lo])).astype(BF16)
                a = _dot_nt(q_t, k_t)
                o_i = o_i + _dot(a.astype(BF16), v16[:lo])
            for s in range(SUB):
                e = jnp.where(sub_row >= s, jnp.exp(bi - bi[s:s + 1]), 0.0)
                w = jnp.sum(qi * e * ki[s:s + 1], axis=-1, keepdims=True)
                o_i = o_i + w * vi[s:s + 1]
            outs.append(o_i)
        o_s[hd, pl.ds(r0, C), :] = jnp.concatenate(outs, axis=0)
        return 0

    lax.fori_loop(0, n_chunks * heads, chunk_head, 0)

    gn = gn_ref[...]
    o = jnp.concatenate([_rms(o_s[hd], gn) for hd in range(heads)], axis=1)
    gate = _silu(_dot(h, w_in_ref[:, 3 * HK:]))
    y = _dot((o * gate).astype(BF16), w_o_ref[...])
    out_ref[...] = _deepnorm_ln(x, y, mod, g_ref[...], b_ref[...])


def hgrn_layer(x, mod, lb_all, layer_idx, w_in, gn, w_o, g, b, B, S, tm):
    N, D = x.shape
    heads = D // HGRN_EXPAND
    nt = S // tm
    row = lambda bb, i: (bb * nt + i, 0)
    full = lambda bb, i: (0, 0)
    hs = pltpu.VMEM((heads, tm, HGRN_EXPAND), F32)
    return pl.pallas_call(
        functools.partial(_hgrn_kernel, layer_idx=layer_idx, heads=heads, tm=tm),
        grid=(B, nt),
        in_specs=[pl.BlockSpec((tm, D), row),
                  pl.BlockSpec((1, 3, D), lambda bb, i: (bb, 0, 0)),
                  pl.BlockSpec(lb_all.shape, full),
                  pl.BlockSpec(w_in.shape, full, pipeline_mode=pl.Buffered(1)),
                  pl.BlockSpec((1, HGRN_EXPAND), full),
                  pl.BlockSpec(w_o.shape, full, pipeline_mode=pl.Buffered(1)),
                  pl.BlockSpec((1, D), full),
                  pl.BlockSpec((1, D), full)],
        out_specs=pl.BlockSpec((tm, D), row),
        out_shape=jax.ShapeDtypeStruct((N, D), F32),
        scratch_shapes=[hs, hs, hs, hs, hs,
                        pltpu.VMEM((heads, HGRN_EXPAND, HGRN_EXPAND), F32)],
        compiler_params=_params("arbitrary", "arbitrary"),
        name="hgrn",
    )(x, mod, lb_all, w_in, gn, w_o, g, b)


def _mla_weights(w_in, w_qb, w_kvb):
    d = w_in.shape[0]
    rope_tail = HEAD_PAD - ROPE_LO - QK_ROPE
    w_in_p = jnp.concatenate(
        [w_in[:, :Q_LORA + KV_LORA], jnp.zeros((d, ROPE_LO), w_in.dtype),
         w_in[:, Q_LORA + KV_LORA:], jnp.zeros((d, rope_tail), w_in.dtype)], axis=1)
    qh = w_qb.reshape(Q_LORA, MLA_HEADS, QK_NOPE + QK_ROPE)
    w_qb_p = jnp.pad(qh, ((0, 0), (0, 0), (0, HEAD_PAD - QK_NOPE - QK_ROPE))).reshape(Q_LORA, -1)
    kvh = w_kvb.reshape(KV_LORA, MLA_HEADS, QK_NOPE + V_HEAD)
    w_k = jnp.pad(kvh[:, :, :QK_NOPE], ((0, 0), (0, 0), (0, HEAD_PAD - QK_NOPE))).reshape(KV_LORA, -1)
    w_v = kvh[:, :, QK_NOPE:].reshape(KV_LORA, -1)
    w_kvb_p = jnp.concatenate([w_k, w_v], axis=1)
    return w_in_p.astype(BF16), w_qb_p.astype(BF16), w_kvb_p.astype(BF16)


def kernel(x, c, positions, mla_w_in, mla_q_norm, mla_w_qb, mla_kv_norm, mla_w_kvb, mla_w_o,
           hgrn_lb, hgrn_w_in, hgrn_g_norm, hgrn_w_o, ffn_w_in, ffn_w_out,
           ada_w, ada_b, ln_g, ln_b):
    B, S, D = x.shape
    depth = ffn_w_in.shape[0]
    tm = min(S, 512)
    tq = min(S, 256)
    t_hgrn = min(S, 256)

    mods = ada_mods(c, ada_w, ada_b)
    ctab, stab = rope_tables(positions)
    xf = x.reshape(B * S, D)

    for layer in range(depth):
        j = layer // N_MIXERS
        mod = mods[2 * layer]
        g0 = ln_g[layer, 0].reshape(1, D)
        b0 = ln_b[layer, 0].reshape(1, D)
        if layer % N_MIXERS == 0:
            w_in_p, w_qb_p, w_kvb_p = _mla_weights(mla_w_in[j], mla_w_qb[j], mla_w_kvb[j])
            q, k, v = mla_proj(xf, mod, w_in_p, mla_q_norm[j].reshape(1, -1), w_qb_p,
                               mla_kv_norm[j].reshape(1, -1), w_kvb_p, ctab, stab, B, S, tm)
            o = mla_attention(q, k, v, B, S, tq)
            xf = proj_ln(o, xf, mod, mla_w_o[j].astype(BF16), g0, b0, B, S, tm)
        else:
            xf = hgrn_layer(xf, mod, hgrn_lb, j, hgrn_w_in[j].astype(BF16),
                            hgrn_g_norm[j].reshape(1, -1), hgrn_w_o[j].astype(BF16),
                            g0, b0, B, S, t_hgrn)
        mod = mods[2 * layer + 1]
        xf = ffn(xf, mod, ffn_w_in[layer].astype(BF16), ffn_w_out[layer].astype(BF16),
                 ln_g[layer, 1].reshape(1, D), ln_b[layer, 1].reshape(1, D), B, S, tm)
    return xf.reshape(B, S, D)
```

```python
import functools

import jax
import jax.numpy as jnp
from jax import lax
from jax.experimental import pallas as pl
from jax.experimental.pallas import tpu as pltpu

F32 = jnp.float32
BF16 = jnp.bfloat16

DEPTH = 4
N_MIXERS = 2

MLA_HEADS = 16
QK_NOPE = 64
QK_ROPE = 32
V_HEAD = 64
Q_LORA = 768
KV_LORA = 256
ROPE_THETA = 10000.0
HEAD_PAD = 128
ROPE_LO = QK_NOPE
ROPE_HALF = QK_ROPE // 2

HGRN_EXPAND = 128
HGRN_CHUNK = 64
HGRN_SUB = 16
ROWS = 8

ALPHA = (2.0 * DEPTH) ** 0.25
LN_EPS = 1e-5
RMS_EPS = 1e-6
LOG2_E = 1.4426950408889634

VMEM_LIMIT_BYTES = 56 * 1024 * 1024


def _params(*semantics):
    return pltpu.CompilerParams(dimension_semantics=semantics,
                                vmem_limit_bytes=VMEM_LIMIT_BYTES)


def _dot(a, b):
    return jnp.dot(a, b, preferred_element_type=F32)


def _dot_nt(a, b):
    return lax.dot_general(a, b, (((1,), (1,)), ((), ())), preferred_element_type=F32)


def _dot_tn(a, b):
    return lax.dot_general(a, b, (((0,), (0,)), ((), ())), preferred_element_type=F32)


def _silu(x):
    return x * jax.nn.sigmoid(x)


def _modulate(x, mod):
    return x * (1.0 + mod[1:2]) + mod[0:1]


def _deepnorm_ln(x, y, mod, g, b):
    z = ALPHA * x + (1.0 + mod[2:3]) * y
    mu = jnp.mean(z, axis=-1, keepdims=True)
    zc = z - mu
    var = jnp.mean(zc * zc, axis=-1, keepdims=True)
    return zc * lax.rsqrt(var + LN_EPS) * g + b


def _rms(x, g):
    ms = jnp.mean(x * x, axis=-1, keepdims=True)
    return x * lax.rsqrt(ms + RMS_EPS) * g


def _ada_kernel(c_ref, w_ref, b_ref, o_ref):
    sc = _silu(c_ref[...]).astype(BF16)
    o_ref[0] = _dot(sc, w_ref[0].astype(BF16)) + b_ref[0]


def ada_mods(c, ada_w, ada_b):
    B, D = c.shape
    L = ada_w.shape[0] * ada_w.shape[1]
    w = ada_w.reshape(L, D, 3 * D)
    b = ada_b.reshape(L, 1, 3 * D)
    out = pl.pallas_call(
        _ada_kernel,
        grid=(L, 3),
        in_specs=[pl.BlockSpec((B, D), lambda l, j: (0, 0)),
                  pl.BlockSpec((1, D, D), lambda l, j: (l, 0, j)),
                  pl.BlockSpec((1, 1, D), lambda l, j: (l, 0, j))],
        out_specs=pl.BlockSpec((1, B, D), lambda l, j: (l, 0, j)),
        out_shape=jax.ShapeDtypeStruct((L, B, 3 * D), F32),
        compiler_params=_params("arbitrary", "arbitrary"),
        name="ada_mods",
    )(c, w, b)
    return out.reshape(L, B, 3, D)


def _trig_kernel(pos_ref, inv_ref, cos_ref, sin_ref):
    ang = pos_ref[...].astype(F32) * inv_ref[...]
    cos_ref[...] = jnp.cos(ang)
    sin_ref[...] = jnp.sin(ang)


def rope_tables(positions):
    n = positions.size
    lanes = 128
    per_row = lanes // ROPE_HALF
    rows = n // per_row
    pos_rep = jnp.repeat(positions.reshape(-1), ROPE_HALF).reshape(rows, lanes)
    inv_freq = ROPE_THETA ** (-jnp.arange(0, QK_ROPE, 2, dtype=F32) / QK_ROPE)
    inv = jnp.tile(inv_freq, per_row).reshape(1, lanes)
    tr = min(rows, 512)
    cos, sin = pl.pallas_call(
        _trig_kernel,
        grid=(rows // tr,),
        in_specs=[pl.BlockSpec((tr, lanes), lambda i: (i, 0)),
                  pl.BlockSpec((1, lanes), lambda i: (0, 0))],
        out_specs=[pl.BlockSpec((tr, lanes), lambda i: (i, 0))] * 2,
        out_shape=[jax.ShapeDtypeStruct((rows, lanes), F32)] * 2,
        compiler_params=_params("arbitrary"),
        name="rope_trig",
    )(pos_rep, inv)
    cos = cos.reshape(n, ROPE_HALF)
    sin = sin.reshape(n, ROPE_HALF)
    tail = HEAD_PAD - ROPE_LO - QK_ROPE
    ctab = jnp.concatenate([jnp.ones((n, ROPE_LO), F32), cos, cos, jnp.ones((n, tail), F32)], axis=1)
    stab = jnp.concatenate([jnp.zeros((n, ROPE_LO), F32), sin, sin, jnp.zeros((n, tail), F32)], axis=1)
    return ctab, stab


def _rope_slot(x, ctab, stab, lane):
    other = jnp.where(lane < ROPE_LO + ROPE_HALF,
                      -pltpu.roll(x, HEAD_PAD - ROPE_HALF, 1),
                      pltpu.roll(x, ROPE_HALF, 1))
    return x * ctab + other * stab


def _mla_proj_kernel(x_ref, mod_ref, w_in_ref, qg_ref, w_qb_ref, kvg_ref, w_kvb_ref,
                     ctab_ref, stab_ref, q_ref, k_ref, v_ref):
    h = _modulate(x_ref[...], mod_ref[0]).astype(BF16)
    proj = _dot(h, w_in_ref[...])
    qn = _rms(proj[:, :Q_LORA], qg_ref[...]).astype(BF16)
    kvn = _rms(proj[:, Q_LORA:Q_LORA + KV_LORA], kvg_ref[...]).astype(BF16)
    kr = proj[:, Q_LORA + KV_LORA:]
    q = _dot(qn, w_qb_ref[...])
    kv = _dot(kvn, w_kvb_ref[...])
    ctab = ctab_ref[...]
    stab = stab_ref[...]
    lane = lax.broadcasted_iota(jnp.int32, ctab.shape, 1)
    kr = _rope_slot(kr, ctab, stab, lane)
    scale = (QK_NOPE + QK_ROPE) ** -0.5 * LOG2_E
    for hd in range(MLA_HEADS):
        sl = slice(hd * HEAD_PAD, (hd + 1) * HEAD_PAD)
        q_ref[:, sl] = (_rope_slot(q[:, sl], ctab, stab, lane) * scale).astype(BF16)
        k_ref[:, sl] = (kv[:, sl] + kr).astype(BF16)
    v_ref[...] = kv[:, MLA_HEADS * HEAD_PAD:].astype(BF16)


def mla_proj(x, mod, w_in_p, qg, w_qb_p, kvg, w_kvb_p, ctab, stab, B, S, tm):
    N, D = x.shape
    nt = S // tm
    row = lambda b, i: (b * nt + i, 0)
    full = lambda b, i: (0, 0)
    HP = MLA_HEADS * HEAD_PAD
    HV = MLA_HEADS * V_HEAD
    return pl.pallas_call(
        _mla_proj_kernel,
        grid=(B, nt),
        in_specs=[pl.BlockSpec((tm, D), row),
                  pl.BlockSpec((1, 3, D), lambda b, i: (b, 0, 0)),
                  pl.BlockSpec(w_in_p.shape, full),
                  pl.BlockSpec(qg.shape, full),
                  pl.BlockSpec(w_qb_p.shape, full),
                  pl.BlockSpec(kvg.shape, full),
                  pl.BlockSpec(w_kvb_p.shape, full),
                  pl.BlockSpec((tm, HEAD_PAD), row),
                  pl.BlockSpec((tm, HEAD_PAD), row)],
        out_specs=[pl.BlockSpec((tm, HP), row),
                   pl.BlockSpec((tm, HP), row),
                   pl.BlockSpec((tm, HV), row)],
        out_shape=[jax.ShapeDtypeStruct((N, HP), BF16),
                   jax.ShapeDtypeStruct((N, HP), BF16),
                   jax.ShapeDtypeStruct((N, HV), BF16)],
        compiler_params=_params("arbitrary", "arbitrary"),
        name="mla_proj",
    )(x, mod, w_in_p, qg, w_qb_p, kvg, w_kvb_p, ctab, stab)


def _attn_kernel(q_ref, k_ref, v_ref, o_ref, *, tq, seq_len):
    tk = tq
    causal = (lax.broadcasted_iota(jnp.int32, (tq, tk), 0)
              >= lax.broadcasted_iota(jnp.int32, (tq, tk), 1))
    first = lax.broadcasted_iota(jnp.int32, (tq, 2 * V_HEAD), 1) < V_HEAD

    def softmax_step(s, m, l):
        m_new = jnp.maximum(m, jnp.max(s, axis=-1, keepdims=True))
        a = jnp.exp2(m - m_new)
        p = jnp.exp2(s - m_new)
        return m_new, a * l + jnp.sum(p, axis=-1, keepdims=True), a, p.astype(BF16)

    for qi in range(seq_len // tq):
        rows = slice(qi * tq, (qi + 1) * tq)
        qa = q_ref[rows, :HEAD_PAD]
        qb = q_ref[rows, HEAD_PAD:]
        ma = mb = jnp.full((tq, 1), -1e30, F32)
        la = lb = jnp.zeros((tq, 1), F32)
        acc = jnp.zeros((tq, 2 * V_HEAD), F32)
        for j in range(qi + 1):
            keys = slice(j * tk, (j + 1) * tk)
            sa = _dot_nt(qa, k_ref[keys, :HEAD_PAD])
            sb = _dot_nt(qb, k_ref[keys, HEAD_PAD:])
            if j == qi:
                sa = jnp.where(causal, sa, -1e30)
                sb = jnp.where(causal, sb, -1e30)
            ma, la, aa, pa = softmax_step(sa, ma, la)
            mb, lb, ab, pb = softmax_step(sb, mb, lb)
            v = v_ref[keys, :]
            acc = (jnp.where(first, aa, ab) * acc
                   + jnp.where(first, _dot(pa, v), _dot(pb, v)))
        o_ref[rows, :] = (acc / jnp.where(first, la, lb)).astype(BF16)


def mla_attention(q, k, v, B, S, tq):
    N = q.shape[0]
    pairs = MLA_HEADS // 2
    HV = MLA_HEADS * V_HEAD
    blk = lambda b, hp: (b, hp)
    return pl.pallas_call(
        functools.partial(_attn_kernel, tq=tq, seq_len=S),
        grid=(B, pairs),
        in_specs=[pl.BlockSpec((S, 2 * HEAD_PAD), blk),
                  pl.BlockSpec((S, 2 * HEAD_PAD), blk),
                  pl.BlockSpec((S, 2 * V_HEAD), blk)],
        out_specs=pl.BlockSpec((S, 2 * V_HEAD), blk),
        out_shape=jax.ShapeDtypeStruct((N, HV), BF16),
        compiler_params=_params("arbitrary", "arbitrary"),
        name="mla_attention",
    )(q, k, v)


def _proj_ln_kernel(o_ref, x_ref, mod_ref, w_ref, g_ref, b_ref, out_ref):
    y = _dot(o_ref[...], w_ref[...])
    out_ref[...] = _deepnorm_ln(x_ref[...], y, mod_ref[0], g_ref[...], b_ref[...])


def proj_ln(o, x, mod, w, g, b, B, S, tm):
    N, D = x.shape
    nt = S // tm
    row = lambda bb, i: (bb * nt + i, 0)
    full = lambda bb, i: (0, 0)
    return pl.pallas_call(
        _proj_ln_kernel,
        grid=(B, nt),
        in_specs=[pl.BlockSpec((tm, o.shape[1]), row),
                  pl.BlockSpec((tm, D), row),
                  pl.BlockSpec((1, 3, D), lambda bb, i: (bb, 0, 0)),
                  pl.BlockSpec(w.shape, full),
                  pl.BlockSpec((1, D), full),
                  pl.BlockSpec((1, D), full)],
        out_specs=pl.BlockSpec((tm, D), row),
        out_shape=jax.ShapeDtypeStruct((N, D), F32),
        compiler_params=_params("arbitrary", "arbitrary"),
        name="proj_ln",
    )(o, x, mod, w, g, b)


def _ffn_kernel(x_ref, mod_ref, w_in_ref, w_out_ref, g_ref, b_ref, out_ref, *, d_ff):
    x = x_ref[...]
    mod = mod_ref[0]
    h = _modulate(x, mod).astype(BF16)
    gate = _dot(h, w_in_ref[:, :d_ff])
    up = _dot(h, w_in_ref[:, d_ff:])
    act = (_silu(gate) * up).astype(BF16)
    y = _dot(act, w_out_ref[...])
    out_ref[...] = _deepnorm_ln(x, y, mod, g_ref[...], b_ref[...])


def ffn(x, mod, w_in, w_out, g, b, B, S, tm):
    N, D = x.shape
    d_ff = w_out.shape[0]
    nt = S // tm
    row = lambda bb, i: (bb * nt + i, 0)
    full = lambda bb, i: (0, 0)
    return pl.pallas_call(
        functools.partial(_ffn_kernel, d_ff=d_ff),
        grid=(B, nt),
        in_specs=[pl.BlockSpec((tm, D), row),
                  pl.BlockSpec((1, 3, D), lambda bb, i: (bb, 0, 0)),
                  pl.BlockSpec(w_in.shape, full, pipeline_mode=pl.Buffered(1)),
                  pl.BlockSpec(w_out.shape, full, pipeline_mode=pl.Buffered(1)),
                  pl.BlockSpec((1, D), full),
                  pl.BlockSpec((1, D), full)],
        out_specs=pl.BlockSpec((tm, D), row),
        out_shape=jax.ShapeDtypeStruct((N, D), F32),
        compiler_params=_params("arbitrary", "arbitrary"),
        name="ffn",
    )(x, mod, w_in, w_out, g, b)


def _split3(x):
    a = x.astype(BF16)
    r = x - a.astype(F32)
    b = r.astype(BF16)
    c = (r - b.astype(F32)).astype(BF16)
    return a, b, c


def _hgrn_kernel(x_ref, mod_ref, lb_ref, w_in_ref, gn_ref, w_o_ref, g_ref, b_ref, out_ref,
                 q_s, k_s, b_s, v_s, o_s, st_s, *, layer_idx, heads, tm):
    C = HGRN_CHUNK
    SUB = HGRN_SUB
    K = HGRN_EXPAND
    HK = heads * K
    n_chunks = tm // C

    @pl.when(pl.program_id(1) == 0)
    def _():
        st_s[...] = jnp.zeros(st_s.shape, F32)

    x = x_ref[...]
    mod = mod_ref[0]
    h = _modulate(x, mod).astype(BF16)

    lb_all = lb_ref[...]
    lb_max = jnp.max(lb_all, axis=0, keepdims=True)
    lb_exp = jnp.exp(lb_all - lb_max)
    lb_soft = lb_exp / jnp.sum(lb_exp, axis=0, keepdims=True)
    lb = jnp.sum(lb_soft[:layer_idx + 1], axis=0, keepdims=True) - lb_soft[0:1]

    q_all = _silu(_dot(h, w_in_ref[:, :HK]))
    f = lb + (1.0 - lb) * jax.nn.sigmoid(_dot(h, w_in_ref[:, HK:2 * HK]))
    logf = jnp.log(f) * LOG2_E
    k_all = 1.0 - f
    v_all = _dot(h, w_in_ref[:, 2 * HK:3 * HK])

    tri = (lax.broadcasted_iota(jnp.int32, (C, C), 0)
           >= lax.broadcasted_iota(jnp.int32, (C, C), 1)).astype(BF16)
    q_s[...] = q_all
    k_s[...] = k_all
    v_s[...] = v_all
    b_s[...] = logf

    grp_row = lax.broadcasted_iota(jnp.int32, (ROWS, K), 0)

    def chunk_head(q, k, b, v, st):
        v16 = v.astype(BF16)
        b_last = b[C - 1:C]
        o_inter = _dot_nt((q * jnp.exp2(b)).astype(BF16), st.astype(BF16))
        k_dec = (k * jnp.exp2(b_last - b)).astype(BF16)
        st_new = st * jnp.exp2(b_last) + _dot_tn(v16, k_dec)
        outs = []
        for i in range(C // SUB):
            lo = i * SUB
            o_i = o_inter[lo:lo + SUB]
            if i > 0:
                ref = b[lo - 1:lo]
                q_t = (q[lo:lo + SUB] * jnp.exp2(b[lo:lo + SUB] - ref)).astype(BF16)
                k_t = (k[:lo] * jnp.exp2(ref - b[:lo])).astype(BF16)
                a = _dot_nt(q_t, k_t)
                o_i = o_i + _dot(a.astype(BF16), v16[:lo])
            for g in range(SUB // ROWS):
                glo = lo + g * ROWS
                bg = b[glo:glo + ROWS]
                qg = q[glo:glo + ROWS]
                o_g = o_i[g * ROWS:(g + 1) * ROWS]
                for s in range(lo, glo + ROWS):
                    e = jnp.exp2(bg - b[s:s + 1])
                    if s >= glo:
                        e = jnp.where(grp_row >= s - glo, e, 0.0)
                    w = jnp.sum(qg * e * k[s:s + 1], axis=-1, keepdims=True)
                    o_g = o_g + w * v[s:s + 1]
                outs.append(o_g)
        return jnp.concatenate(outs, axis=0), st_new

    def chunk(c, _):
        rows = pl.ds(pl.multiple_of(c * C, C), C)
        g1, g2, g3 = _split3(b_s[rows, :])
        bc = _dot(tri, g1) + _dot(tri, g2) + _dot(tri, g3)
        for hd in range(heads):
            sl = slice(hd * K, (hd + 1) * K)
            o_h, st_new = chunk_head(q_s[rows, sl], k_s[rows, sl], bc[:, sl], v_s[rows, sl], st_s[hd])
            st_s[hd] = st_new
            o_s[rows, sl] = o_h
        return 0

    lax.fori_loop(0, n_chunks, chunk, 0)

    gn = gn_ref[...]
    o = jnp.concatenate([_rms(o_s[:, hd * K:(hd + 1) * K], gn) for hd in range(heads)], axis=1)
    gate = _silu(_dot(h, w_in_ref[:, 3 * HK:]))
    y = _dot((o * gate).astype(BF16), w_o_ref[...])
    out_ref[...] = _deepnorm_ln(x, y, mod, g_ref[...], b_ref[...])


def hgrn_layer(x, mod, lb_all, layer_idx, w_in, gn, w_o, g, b, B, S, tm):
    N, D = x.shape
    heads = D // HGRN_EXPAND
    nt = S // tm
    row = lambda bb, i: (bb * nt + i, 0)
    full = lambda bb, i: (0, 0)
    hs = pltpu.VMEM((tm, D), F32)
    return pl.pallas_call(
        functools.partial(_hgrn_kernel, layer_idx=layer_idx, heads=heads, tm=tm),
        grid=(B, nt),
        in_specs=[pl.BlockSpec((tm, D), row),
                  pl.BlockSpec((1, 3, D), lambda bb, i: (bb, 0, 0)),
                  pl.BlockSpec(lb_all.shape, full),
                  pl.BlockSpec(w_in.shape, full, pipeline_mode=pl.Buffered(1)),
                  pl.BlockSpec((1, HGRN_EXPAND), full),
                  pl.BlockSpec(w_o.shape, full, pipeline_mode=pl.Buffered(1)),
                  pl.BlockSpec((1, D), full),
                  pl.BlockSpec((1, D), full)],
        out_specs=pl.BlockSpec((tm, D), row),
        out_shape=jax.ShapeDtypeStruct((N, D), F32),
        scratch_shapes=[hs, hs, hs, hs, hs,
                        pltpu.VMEM((heads, HGRN_EXPAND, HGRN_EXPAND), F32)],
        compiler_params=_params("arbitrary", "arbitrary"),
        name="hgrn",
    )(x, mod, lb_all, w_in, gn, w_o, g, b)


def _mla_weights(w_in, w_qb, w_kvb):
    d = w_in.shape[0]
    rope_tail = HEAD_PAD - ROPE_LO - QK_ROPE
    w_in_p = jnp.concatenate(
        [w_in[:, :Q_LORA + KV_LORA], jnp.zeros((d, ROPE_LO), w_in.dtype),
         w_in[:, Q_LORA + KV_LORA:], jnp.zeros((d, rope_tail), w_in.dtype)], axis=1)
    qh = w_qb.reshape(Q_LORA, MLA_HEADS, QK_NOPE + QK_ROPE)
    w_qb_p = jnp.pad(qh, ((0, 0), (0, 0), (0, HEAD_PAD - QK_NOPE - QK_ROPE))).reshape(Q_LORA, -1)
    kvh = w_kvb.reshape(KV_LORA, MLA_HEADS, QK_NOPE + V_HEAD)
    w_k = jnp.pad(kvh[:, :, :QK_NOPE], ((0, 0), (0, 0), (0, HEAD_PAD - QK_NOPE))).reshape(KV_LORA, -1)
    w_v = kvh[:, :, QK_NOPE:].reshape(KV_LORA, -1)
    w_kvb_p = jnp.concatenate([w_k, w_v], axis=1)
    return w_in_p.astype(BF16), w_qb_p.astype(BF16), w_kvb_p.astype(BF16)


def kernel(x, c, positions, mla_w_in, mla_q_norm, mla_w_qb, mla_kv_norm, mla_w_kvb, mla_w_o,
           hgrn_lb, hgrn_w_in, hgrn_g_norm, hgrn_w_o, ffn_w_in, ffn_w_out,
           ada_w, ada_b, ln_g, ln_b):
    B, S, D = x.shape
    depth = ffn_w_in.shape[0]
    tm = min(S, 512)
    tq = min(S, 512)
    t_hgrn = min(S, 512)

    mods = ada_mods(c, ada_w, ada_b)
    ctab, stab = rope_tables(positions)
    xf = x.reshape(B * S, D)

    for layer in range(depth):
        j = layer // N_MIXERS
        mod = mods[2 * layer]
        g0 = ln_g[layer, 0].reshape(1, D)
        b0 = ln_b[layer, 0].reshape(1, D)
        if layer % N_MIXERS == 0:
            w_in_p, w_qb_p, w_kvb_p = _mla_weights(mla_w_in[j], mla_w_qb[j], mla_w_kvb[j])
            q, k, v = mla_proj(xf, mod, w_in_p, mla_q_norm[j].reshape(1, -1), w_qb_p,
                               mla_kv_norm[j].reshape(1, -1), w_kvb_p, ctab, stab, B, S, tm)
            o = mla_attention(q, k, v, B, S, tq)
            xf = proj_ln(o, xf, mod, mla_w_o[j].astype(BF16), g0, b0, B, S, tm)
        else:
            xf = hgrn_layer(xf, mod, hgrn_lb, j, hgrn_w_in[j].astype(BF16),
                            hgrn_g_norm[j].reshape(1, -1), hgrn_w_o[j].astype(BF16),
                            g0, b0, B, S, t_hgrn)
        mod = mods[2 * layer + 1]
        xf = ffn(xf, mod, ffn_w_in[layer].astype(BF16), ffn_w_out[layer].astype(BF16),
                 ln_g[layer, 1].reshape(1, D), ln_b[layer, 1].reshape(1, D), B, S, tm)
    return xf.reshape(B, S, D)
```

```python
import functools

import jax
import jax.numpy as jnp
from jax import lax
from jax.experimental import pallas as pl
from jax.experimental.pallas import tpu as pltpu

F32 = jnp.float32
BF16 = jnp.bfloat16

DEPTH = 4
N_MIXERS = 2

MLA_HEADS = 16
QK_NOPE = 64
QK_ROPE = 32
V_HEAD = 64
Q_LORA = 768
KV_LORA = 256
ROPE_THETA = 10000.0
HEAD_PAD = 128
ROPE_LO = QK_NOPE
ROPE_HALF = QK_ROPE // 2

HGRN_EXPAND = 128
HGRN_CHUNK = 64
HGRN_SUB = 16
ROWS = 8
MAX_LOG2_SPAN = 100.0

ALPHA = (2.0 * DEPTH) ** 0.25
LN_EPS = 1e-5
RMS_EPS = 1e-6
LOG2_E = 1.4426950408889634

VMEM_LIMIT_BYTES = 56 * 1024 * 1024


def _params(*semantics):
    return pltpu.CompilerParams(dimension_semantics=semantics,
                                vmem_limit_bytes=VMEM_LIMIT_BYTES)


def _dot(a, b):
    return jnp.dot(a, b, preferred_element_type=F32)


def _dot_nt(a, b):
    return lax.dot_general(a, b, (((1,), (1,)), ((), ())), preferred_element_type=F32)


def _dot_tn(a, b):
    return lax.dot_general(a, b, (((0,), (0,)), ((), ())), preferred_element_type=F32)


def _silu(x):
    return x * jax.nn.sigmoid(x)


def _modulate(x, mod):
    return x * (1.0 + mod[1:2]) + mod[0:1]


def _deepnorm_ln(x, y, mod, g, b):
    z = ALPHA * x + (1.0 + mod[2:3]) * y
    mu = jnp.mean(z, axis=-1, keepdims=True)
    zc = z - mu
    var = jnp.mean(zc * zc, axis=-1, keepdims=True)
    return zc * lax.rsqrt(var + LN_EPS) * g + b


def _rms(x, g):
    ms = jnp.mean(x * x, axis=-1, keepdims=True)
    return x * lax.rsqrt(ms + RMS_EPS) * g


def _ada_kernel(c_ref, w_ref, b_ref, o_ref):
    sc = _silu(c_ref[...]).astype(BF16)
    o_ref[0] = _dot(sc, w_ref[0].astype(BF16)) + b_ref[0]


def ada_mods(c, ada_w, ada_b):
    B, D = c.shape
    L = ada_w.shape[0] * ada_w.shape[1]
    w = ada_w.reshape(L, D, 3 * D)
    b = ada_b.reshape(L, 1, 3 * D)
    out = pl.pallas_call(
        _ada_kernel,
        grid=(L, 3),
        in_specs=[pl.BlockSpec((B, D), lambda l, j: (0, 0)),
                  pl.BlockSpec((1, D, D), lambda l, j: (l, 0, j)),
                  pl.BlockSpec((1, 1, D), lambda l, j: (l, 0, j))],
        out_specs=pl.BlockSpec((1, B, D), lambda l, j: (l, 0, j)),
        out_shape=jax.ShapeDtypeStruct((L, B, 3 * D), F32),
        compiler_params=_params("arbitrary", "arbitrary"),
        name="ada_mods",
    )(c, w, b)
    return out.reshape(L, B, 3, D)


def _trig_kernel(pos_ref, inv_ref, cos_ref, sin_ref):
    ang = pos_ref[...].astype(F32) * inv_ref[...]
    cos_ref[...] = jnp.cos(ang)
    sin_ref[...] = jnp.sin(ang)


def rope_tables(positions):
    n = positions.size
    lanes = 128
    per_row = lanes // ROPE_HALF
    rows = n // per_row
    pos_rep = jnp.repeat(positions.reshape(-1), ROPE_HALF).reshape(rows, lanes)
    inv_freq = ROPE_THETA ** (-jnp.arange(0, QK_ROPE, 2, dtype=F32) / QK_ROPE)
    inv = jnp.tile(inv_freq, per_row).reshape(1, lanes)
    tr = min(rows, 512)
    cos, sin = pl.pallas_call(
        _trig_kernel,
        grid=(rows // tr,),
        in_specs=[pl.BlockSpec((tr, lanes), lambda i: (i, 0)),
                  pl.BlockSpec((1, lanes), lambda i: (0, 0))],
        out_specs=[pl.BlockSpec((tr, lanes), lambda i: (i, 0))] * 2,
        out_shape=[jax.ShapeDtypeStruct((rows, lanes), F32)] * 2,
        compiler_params=_params("arbitrary"),
        name="rope_trig",
    )(pos_rep, inv)
    cos = cos.reshape(n, ROPE_HALF)
    sin = sin.reshape(n, ROPE_HALF)
    tail = HEAD_PAD - ROPE_LO - QK_ROPE
    ctab = jnp.concatenate([jnp.ones((n, ROPE_LO), F32), cos, cos, jnp.ones((n, tail), F32)], axis=1)
    stab = jnp.concatenate([jnp.zeros((n, ROPE_LO), F32), sin, sin, jnp.zeros((n, tail), F32)], axis=1)
    return ctab, stab


def _rope_slot(x, ctab, stab, lane):
    other = jnp.where(lane < ROPE_LO + ROPE_HALF,
                      -pltpu.roll(x, HEAD_PAD - ROPE_HALF, 1),
                      pltpu.roll(x, ROPE_HALF, 1))
    return x * ctab + other * stab


def _mla_proj_kernel(x_ref, mod_ref, w_in_ref, qg_ref, w_qb_ref, kvg_ref, w_kvb_ref,
                     ctab_ref, stab_ref, q_ref, k_ref, v_ref):
    h = _modulate(x_ref[...], mod_ref[0]).astype(BF16)
    proj = _dot(h, w_in_ref[...])
    qn = _rms(proj[:, :Q_LORA], qg_ref[...]).astype(BF16)
    kvn = _rms(proj[:, Q_LORA:Q_LORA + KV_LORA], kvg_ref[...]).astype(BF16)
    kr = proj[:, Q_LORA + KV_LORA:]
    q = _dot(qn, w_qb_ref[...])
    kv = _dot(kvn, w_kvb_ref[...])
    ctab = ctab_ref[...]
    stab = stab_ref[...]
    lane = lax.broadcasted_iota(jnp.int32, ctab.shape, 1)
    kr = _rope_slot(kr, ctab, stab, lane)
    scale = (QK_NOPE + QK_ROPE) ** -0.5 * LOG2_E
    for hd in range(MLA_HEADS):
        sl = slice(hd * HEAD_PAD, (hd + 1) * HEAD_PAD)
        q_ref[:, sl] = (_rope_slot(q[:, sl], ctab, stab, lane) * scale).astype(BF16)
        k_ref[:, sl] = (kv[:, sl] + kr).astype(BF16)
    v_ref[...] = kv[:, MLA_HEADS * HEAD_PAD:].astype(BF16)


def mla_proj(x, mod, w_in_p, qg, w_qb_p, kvg, w_kvb_p, ctab, stab, B, S, tm):
    N, D = x.shape
    nt = S // tm
    row = lambda b, i: (b * nt + i, 0)
    full = lambda b, i: (0, 0)
    HP = MLA_HEADS * HEAD_PAD
    HV = MLA_HEADS * V_HEAD
    return pl.pallas_call(
        _mla_proj_kernel,
        grid=(B, nt),
        in_specs=[pl.BlockSpec((tm, D), row),
                  pl.BlockSpec((1, 3, D), lambda b, i: (b, 0, 0)),
                  pl.BlockSpec(w_in_p.shape, full),
                  pl.BlockSpec(qg.shape, full),
                  pl.BlockSpec(w_qb_p.shape, full),
                  pl.BlockSpec(kvg.shape, full),
                  pl.BlockSpec(w_kvb_p.shape, full),
                  pl.BlockSpec((tm, HEAD_PAD), row),
                  pl.BlockSpec((tm, HEAD_PAD), row)],
        out_specs=[pl.BlockSpec((tm, HP), row),
                   pl.BlockSpec((tm, HP), row),
                   pl.BlockSpec((tm, HV), row)],
        out_shape=[jax.ShapeDtypeStruct((N, HP), BF16),
                   jax.ShapeDtypeStruct((N, HP), BF16),
                   jax.ShapeDtypeStruct((N, HV), BF16)],
        compiler_params=_params("arbitrary", "arbitrary"),
        name="mla_proj",
    )(x, mod, w_in_p, qg, w_qb_p, kvg, w_kvb_p, ctab, stab)


def _attn_kernel(q_ref, k_ref, v_ref, o_ref, *, tq, seq_len):
    causal = (lax.broadcasted_iota(jnp.int32, (tq, tq), 0)
              >= lax.broadcasted_iota(jnp.int32, (tq, tq), 1))
    first = lax.broadcasted_iota(jnp.int32, (tq, 2 * V_HEAD), 1) < V_HEAD

    def head_rows(q, kcols, q0):
        n = q0 + tq
        s_diag = jnp.where(causal, _dot_nt(q, k_ref[q0:n, kcols]), -1e30)
        m = jnp.max(s_diag, axis=-1, keepdims=True)
        if q0 > 0:
            s_prev = _dot_nt(q, k_ref[:q0, kcols])
            m = jnp.maximum(m, jnp.max(s_prev, axis=-1, keepdims=True))
        p_diag = jnp.exp2(s_diag - m)
        l = jnp.sum(p_diag, axis=-1, keepdims=True)
        o = _dot(p_diag.astype(BF16), v_ref[q0:n, :])
        if q0 > 0:
            p_prev = jnp.exp2(s_prev - m)
            l = l + jnp.sum(p_prev, axis=-1, keepdims=True)
            o = o + _dot(p_prev.astype(BF16), v_ref[:q0, :])
        return o / l

    for qi in range(seq_len // tq):
        q0 = qi * tq
        oa = head_rows(q_ref[q0:q0 + tq, :HEAD_PAD], slice(0, HEAD_PAD), q0)
        ob = head_rows(q_ref[q0:q0 + tq, HEAD_PAD:], slice(HEAD_PAD, 2 * HEAD_PAD), q0)
        o_ref[q0:q0 + tq, :] = jnp.where(first, oa, ob).astype(BF16)


def mla_attention(q, k, v, B, S, tq):
    N = q.shape[0]
    pairs = MLA_HEADS // 2
    HV = MLA_HEADS * V_HEAD
    blk = lambda b, hp: (b, hp)
    return pl.pallas_call(
        functools.partial(_attn_kernel, tq=tq, seq_len=S),
        grid=(B, pairs),
        in_specs=[pl.BlockSpec((S, 2 * HEAD_PAD), blk),
                  pl.BlockSpec((S, 2 * HEAD_PAD), blk),
                  pl.BlockSpec((S, 2 * V_HEAD), blk)],
        out_specs=pl.BlockSpec((S, 2 * V_HEAD), blk),
        out_shape=jax.ShapeDtypeStruct((N, HV), BF16),
        compiler_params=_params("arbitrary", "arbitrary"),
        name="mla_attention",
    )(q, k, v)


def _proj_ln_kernel(o_ref, x_ref, mod_ref, w_ref, g_ref, b_ref, out_ref):
    y = _dot(o_ref[...], w_ref[...])
    out_ref[...] = _deepnorm_ln(x_ref[...], y, mod_ref[0], g_ref[...], b_ref[...])


def proj_ln(o, x, mod, w, g, b, B, S, tm):
    N, D = x.shape
    nt = S // tm
    row = lambda bb, i: (bb * nt + i, 0)
    full = lambda bb, i: (0, 0)
    return pl.pallas_call(
        _proj_ln_kernel,
        grid=(B, nt),
        in_specs=[pl.BlockSpec((tm, o.shape[1]), row),
                  pl.BlockSpec((tm, D), row),
                  pl.BlockSpec((1, 3, D), lambda bb, i: (bb, 0, 0)),
                  pl.BlockSpec(w.shape, full),
                  pl.BlockSpec((1, D), full),
                  pl.BlockSpec((1, D), full)],
        out_specs=pl.BlockSpec((tm, D), row),
        out_shape=jax.ShapeDtypeStruct((N, D), F32),
        compiler_params=_params("arbitrary", "arbitrary"),
        name="proj_ln",
    )(o, x, mod, w, g, b)


def _ffn_kernel(x_ref, mod_ref, w_in_ref, w_out_ref, g_ref, b_ref, out_ref, *, d_ff):
    x = x_ref[...]
    mod = mod_ref[0]
    h = _modulate(x, mod).astype(BF16)
    gate = _dot(h, w_in_ref[:, :d_ff])
    up = _dot(h, w_in_ref[:, d_ff:])
    act = (_silu(gate) * up).astype(BF16)
    y = _dot(act, w_out_ref[...])
    out_ref[...] = _deepnorm_ln(x, y, mod, g_ref[...], b_ref[...])


def ffn(x, mod, w_in, w_out, g, b, B, S, tm):
    N, D = x.shape
    d_ff = w_out.shape[0]
    nt = S // tm
    row = lambda bb, i: (bb * nt + i, 0)
    full = lambda bb, i: (0, 0)
    return pl.pallas_call(
        functools.partial(_ffn_kernel, d_ff=d_ff),
        grid=(B, nt),
        in_specs=[pl.BlockSpec((tm, D), row),
                  pl.BlockSpec((1, 3, D), lambda bb, i: (bb, 0, 0)),
                  pl.BlockSpec(w_in.shape, full, pipeline_mode=pl.Buffered(1)),
                  pl.BlockSpec(w_out.shape, full, pipeline_mode=pl.Buffered(1)),
                  pl.BlockSpec((1, D), full),
                  pl.BlockSpec((1, D), full)],
        out_specs=pl.BlockSpec((tm, D), row),
        out_shape=jax.ShapeDtypeStruct((N, D), F32),
        compiler_params=_params("arbitrary", "arbitrary"),
        name="ffn",
    )(x, mod, w_in, w_out, g, b)


def _split3(x):
    a = x.astype(BF16)
    r = x - a.astype(F32)
    b = r.astype(BF16)
    c = (r - b.astype(F32)).astype(BF16)
    return a, b, c


def _hgrn_kernel(x_ref, mod_ref, lb_ref, w_in_ref, gn_ref, w_o_ref, g_ref, b_ref, out_ref,
                 q_s, k_s, b_s, v_s, o_s, st_s, *, layer_idx, heads, tm):
    C = HGRN_CHUNK
    SUB = HGRN_SUB
    K = HGRN_EXPAND
    HK = heads * K
    n_chunks = tm // C

    @pl.when(pl.program_id(1) == 0)
    def _():
        st_s[...] = jnp.zeros(st_s.shape, F32)

    x = x_ref[...]
    mod = mod_ref[0]
    h = _modulate(x, mod).astype(BF16)

    lb_all = lb_ref[...]
    lb_max = jnp.max(lb_all, axis=0, keepdims=True)
    lb_exp = jnp.exp(lb_all - lb_max)
    lb_soft = lb_exp / jnp.sum(lb_exp, axis=0, keepdims=True)
    lb = jnp.sum(lb_soft[:layer_idx + 1], axis=0, keepdims=True) - lb_soft[0:1]

    q_all = _silu(_dot(h, w_in_ref[:, :HK]))
    f = lb + (1.0 - lb) * jax.nn.sigmoid(_dot(h, w_in_ref[:, HK:2 * HK]))
    logf = jnp.log(f) * LOG2_E
    k_all = 1.0 - f
    v_all = _dot(h, w_in_ref[:, 2 * HK:3 * HK])

    tri = (lax.broadcasted_iota(jnp.int32, (C, C), 0)
           >= lax.broadcasted_iota(jnp.int32, (C, C), 1)).astype(BF16)
    MID = C // 2
    half_span = jnp.zeros((1, HK), F32)
    for c in range(n_chunks):
        rows = slice(c * C, (c + 1) * C)
        g1, g2, g3 = _split3(logf[rows])
        bc = _dot(tri, g1) + _dot(tri, g2) + _dot(tri, g3)
        b_s[rows, :] = bc
        half_span = jnp.maximum(half_span, jnp.maximum(bc[0:1] - bc[MID:MID + 1],
                                                       bc[MID:MID + 1] - bc[C - 1:C]))
    q_s[...] = q_all
    k_s[...] = k_all
    v_s[...] = v_all
    factorable = jnp.max(half_span) <= MAX_LOG2_SPAN
    sls = [slice(hd * K, (hd + 1) * K) for hd in range(heads)]

    @pl.when(factorable)
    def _():
        causal = (lax.broadcasted_iota(jnp.int32, (C, C), 0)
                  >= lax.broadcasted_iota(jnp.int32, (C, C), 1))
        states = [st_s[hd] for hd in range(heads)]
        for c in range(n_chunks):
            rows = slice(c * C, (c + 1) * C)
            b = b_s[rows, :]
            q = q_s[rows, :]
            k = k_s[rows, :]
            v16 = v_s[rows, :].astype(BF16)
            ref = b[MID:MID + 1]
            b_last = b[C - 1:C]
            q_t = (q * jnp.exp2(b - ref)).astype(BF16)
            k_t = (k * jnp.exp2(ref - b)).astype(BF16)
            q_e = (q * jnp.exp2(b)).astype(BF16)
            k_d = (k * jnp.exp2(b_last - b)).astype(BF16)
            decay = jnp.exp2(b_last)
            outs = []
            for hd, sl in enumerate(sls):
                a = jnp.where(causal, _dot_nt(q_t[:, sl], k_t[:, sl]), 0.0).astype(BF16)
                st = states[hd]
                outs.append(_dot(a, v16[:, sl]) + _dot_nt(q_e[:, sl], st.astype(BF16)))
                states[hd] = st * decay[:, sl] + _dot_tn(v16[:, sl], k_d[:, sl])
            o_s[rows, :] = jnp.concatenate(outs, axis=1)
        for hd in range(heads):
            st_s[hd] = states[hd]

    grp_row = lax.broadcasted_iota(jnp.int32, (ROWS, K), 0)

    def chunk_head_direct(q, k, b, v, st):
        v16 = v.astype(BF16)
        b_last = b[C - 1:C]
        o_inter = _dot_nt((q * jnp.exp2(b)).astype(BF16), st.astype(BF16))
        k_dec = (k * jnp.exp2(b_last - b)).astype(BF16)
        st_new = st * jnp.exp2(b_last) + _dot_tn(v16, k_dec)
        outs = []
        for i in range(C // SUB):
            lo = i * SUB
            o_i = o_inter[lo:lo + SUB]
            if i > 0:
                ref = b[lo - 1:lo]
                q_t = (q[lo:lo + SUB] * jnp.exp2(b[lo:lo + SUB] - ref)).astype(BF16)
                k_t = (k[:lo] * jnp.exp2(ref - b[:lo])).astype(BF16)
                a = _dot_nt(q_t, k_t)
                o_i = o_i + _dot(a.astype(BF16), v16[:lo])
            for g in range(SUB // ROWS):
                glo = lo + g * ROWS
                bg = b[glo:glo + ROWS]
                qg = q[glo:glo + ROWS]
                o_g = o_i[g * ROWS:(g + 1) * ROWS]
                for s in range(lo, glo + ROWS):
                    e = jnp.exp2(bg - b[s:s + 1])
                    if s >= glo:
                        e = jnp.where(grp_row >= s - glo, e, 0.0)
                    w = jnp.sum(qg * e * k[s:s + 1], axis=-1, keepdims=True)
                    o_g = o_g + w * v[s:s + 1]
                outs.append(o_g)
        return jnp.concatenate(outs, axis=0), st_new

    @pl.when(jnp.logical_not(factorable))
    def _():
        def chunk(c, _):
            rows = pl.ds(pl.multiple_of(c * C, C), C)
            for hd, sl in enumerate(sls):
                o_h, st_new = chunk_head_direct(q_s[rows, sl], k_s[rows, sl], b_s[rows, sl],
                                                v_s[rows, sl], st_s[hd])
                st_s[hd] = st_new
                o_s[rows, sl] = o_h
            return 0

        lax.fori_loop(0, n_chunks, chunk, 0)

    gn = gn_ref[...]
    o = jnp.concatenate([_rms(o_s[:, hd * K:(hd + 1) * K], gn) for hd in range(heads)], axis=1)
    gate = _silu(_dot(h, w_in_ref[:, 3 * HK:]))
    y = _dot((o * gate).astype(BF16), w_o_ref[...])
    out_ref[...] = _deepnorm_ln(x, y, mod, g_ref[...], b_ref[...])


def hgrn_layer(x, mod, lb_all, layer_idx, w_in, gn, w_o, g, b, B, S, tm):
    N, D = x.shape
    heads = D // HGRN_EXPAND
    nt = S // tm
    row = lambda bb, i: (bb * nt + i, 0)
    full = lambda bb, i: (0, 0)
    hs = pltpu.VMEM((tm, D), F32)
    return pl.pallas_call(
        functools.partial(_hgrn_kernel, layer_idx=layer_idx, heads=heads, tm=tm),
        grid=(B, nt),
        in_specs=[pl.BlockSpec((tm, D), row),
                  pl.BlockSpec((1, 3, D), lambda bb, i: (bb, 0, 0)),
                  pl.BlockSpec(lb_all.shape, full),
                  pl.BlockSpec(w_in.shape, full, pipeline_mode=pl.Buffered(1)),
                  pl.BlockSpec((1, HGRN_EXPAND), full),
                  pl.BlockSpec(w_o.shape, full, pipeline_mode=pl.Buffered(1)),
                  pl.BlockSpec((1, D), full),
                  pl.BlockSpec((1, D), full)],
        out_specs=pl.BlockSpec((tm, D), row),
        out_shape=jax.ShapeDtypeStruct((N, D), F32),
        scratch_shapes=[hs, hs, hs, hs, hs,
                        pltpu.VMEM((heads, HGRN_EXPAND, HGRN_EXPAND), F32)],
        compiler_params=_params("arbitrary", "arbitrary"),
        name="hgrn",
    )(x, mod, lb_all, w_in, gn, w_o, g, b)


def _mla_weights(w_in, w_qb, w_kvb):
    d = w_in.shape[0]
    rope_tail = HEAD_PAD - ROPE_LO - QK_ROPE
    w_in_p = jnp.concatenate(
        [w_in[:, :Q_LORA + KV_LORA], jnp.zeros((d, ROPE_LO), w_in.dtype),
         w_in[:, Q_LORA + KV_LORA:], jnp.zeros((d, rope_tail), w_in.dtype)], axis=1)
    qh = w_qb.reshape(Q_LORA, MLA_HEADS, QK_NOPE + QK_ROPE)
    w_qb_p = jnp.pad(qh, ((0, 0), (0, 0), (0, HEAD_PAD - QK_NOPE - QK_ROPE))).reshape(Q_LORA, -1)
    kvh = w_kvb.reshape(KV_LORA, MLA_HEADS, QK_NOPE + V_HEAD)
    w_k = jnp.pad(kvh[:, :, :QK_NOPE], ((0, 0), (0, 0), (0, HEAD_PAD - QK_NOPE))).reshape(KV_LORA, -1)
    w_v = kvh[:, :, QK_NOPE:].reshape(KV_LORA, -1)
    w_kvb_p = jnp.concatenate([w_k, w_v], axis=1)
    return w_in_p.astype(BF16), w_qb_p.astype(BF16), w_kvb_p.astype(BF16)


def kernel(x, c, positions, mla_w_in, mla_q_norm, mla_w_qb, mla_kv_norm, mla_w_kvb, mla_w_o,
           hgrn_lb, hgrn_w_in, hgrn_g_norm, hgrn_w_o, ffn_w_in, ffn_w_out,
           ada_w, ada_b, ln_g, ln_b):
    B, S, D = x.shape
    depth = ffn_w_in.shape[0]
    tm = min(S, 512)
    tq = min(S, 512)
    t_hgrn = min(S, 512)

    mods = ada_mods(c, ada_w, ada_b)
    ctab, stab = rope_tables(positions)
    xf = x.reshape(B * S, D)

    for layer in range(depth):
        j = layer // N_MIXERS
        mod = mods[2 * layer]
        g0 = ln_g[layer, 0].reshape(1, D)
        b0 = ln_b[layer, 0].reshape(1, D)
        if layer % N_MIXERS == 0:
            w_in_p, w_qb_p, w_kvb_p = _mla_weights(mla_w_in[j], mla_w_qb[j], mla_w_kvb[j])
            q, k, v = mla_proj(xf, mod, w_in_p, mla_q_norm[j].reshape(1, -1), w_qb_p,
                               mla_kv_norm[j].reshape(1, -1), w_kvb_p, ctab, stab, B, S, tm)
            o = mla_attention(q, k, v, B, S, tq)
            xf = proj_ln(o, xf, mod, mla_w_o[j].astype(BF16), g0, b0, B, S, tm)
        else:
            xf = hgrn_layer(xf, mod, hgrn_lb, j, hgrn_w_in[j].astype(BF16),
                            hgrn_g_norm[j].reshape(1, -1), hgrn_w_o[j].astype(BF16),
                            g0, b0, B, S, t_hgrn)
        mod = mods[2 * layer + 1]
        xf = ffn(xf, mod, ffn_w_in[layer].astype(BF16), ffn_w_out[layer].astype(BF16),
                 ln_g[layer, 1].reshape(1, D), ln_b[layer, 1].reshape(1, D), B, S, tm)
    return xf.reshape(B, S, D)
```

```python
import functools

import jax
import jax.numpy as jnp
from jax import lax
from jax.experimental import pallas as pl
from jax.experimental.pallas import tpu as pltpu

F32 = jnp.float32
BF16 = jnp.bfloat16

DEPTH = 4
N_MIXERS = 2

MLA_HEADS = 16
QK_NOPE = 64
QK_ROPE = 32
V_HEAD = 64
Q_LORA = 768
KV_LORA = 256
ROPE_THETA = 10000.0
HEAD_PAD = 128
ROPE_LO = QK_NOPE
ROPE_HALF = QK_ROPE // 2

HGRN_EXPAND = 128
HGRN_CHUNK = 64
HGRN_SUB = 16
ROWS = 8
MAX_LOG2_SPAN = 100.0
LN_ROW_GROUP = 256

ALPHA = (2.0 * DEPTH) ** 0.25
LN_EPS = 1e-5
RMS_EPS = 1e-6
LOG2_E = 1.4426950408889634

VMEM_LIMIT_BYTES = 56 * 1024 * 1024


def _params(*semantics):
    return pltpu.CompilerParams(dimension_semantics=semantics,
                                vmem_limit_bytes=VMEM_LIMIT_BYTES)


def _dot(a, b):
    return jnp.dot(a, b, preferred_element_type=F32)


def _dot_nt(a, b):
    return lax.dot_general(a, b, (((1,), (1,)), ((), ())), preferred_element_type=F32)


def _dot_tn(a, b):
    return lax.dot_general(a, b, (((0,), (0,)), ((), ())), preferred_element_type=F32)


def _silu(x):
    return x * jax.nn.sigmoid(x)


def _modulate(x, mod):
    return x * (1.0 + mod[1:2]) + mod[0:1]


def _deepnorm_ln(x, y, mod, g, b):
    z = ALPHA * x + (1.0 + mod[2:3]) * y
    mu = jnp.mean(z, axis=-1, keepdims=True)
    zc = z - mu
    var = jnp.mean(zc * zc, axis=-1, keepdims=True)
    return zc * lax.rsqrt(var + LN_EPS) * g + b


def _rms(x, g):
    ms = jnp.mean(x * x, axis=-1, keepdims=True)
    return x * lax.rsqrt(ms + RMS_EPS) * g


def _proj_ln_rows(a, x, mod, w_ref, g, b, out_ref):
    rows = a.shape[0]
    step = min(rows, LN_ROW_GROUP)
    for r0 in range(0, rows, step):
        r = slice(r0, r0 + step)
        out_ref[r, :] = _deepnorm_ln(x[r], _dot(a[r], w_ref[...]), mod, g, b)


def _ada_kernel(c_ref, w_ref, b_ref, o_ref):
    sc = _silu(c_ref[...]).astype(BF16)
    o_ref[0] = _dot(sc, w_ref[0].astype(BF16)) + b_ref[0]


def ada_mods(c, ada_w, ada_b):
    B, D = c.shape
    L = ada_w.shape[0] * ada_w.shape[1]
    w = ada_w.reshape(L, D, 3 * D)
    b = ada_b.reshape(L, 1, 3 * D)
    out = pl.pallas_call(
        _ada_kernel,
        grid=(L, 3),
        in_specs=[pl.BlockSpec((B, D), lambda l, j: (0, 0)),
                  pl.BlockSpec((1, D, D), lambda l, j: (l, 0, j)),
                  pl.BlockSpec((1, 1, D), lambda l, j: (l, 0, j))],
        out_specs=pl.BlockSpec((1, B, D), lambda l, j: (l, 0, j)),
        out_shape=jax.ShapeDtypeStruct((L, B, 3 * D), F32),
        compiler_params=_params("arbitrary", "arbitrary"),
        name="ada_mods",
    )(c, w, b)
    return out.reshape(L, B, 3, D)


def _trig_kernel(pos_ref, inv_ref, cos_ref, sin_ref):
    ang = pos_ref[...].astype(F32) * inv_ref[...]
    cos_ref[...] = jnp.cos(ang)
    sin_ref[...] = jnp.sin(ang)


def rope_tables(positions):
    n = positions.size
    lanes = 128
    per_row = lanes // ROPE_HALF
    rows = n // per_row
    pos_rep = jnp.repeat(positions.reshape(-1), ROPE_HALF).reshape(rows, lanes)
    inv_freq = ROPE_THETA ** (-jnp.arange(0, QK_ROPE, 2, dtype=F32) / QK_ROPE)
    inv = jnp.tile(inv_freq, per_row).reshape(1, lanes)
    tr = min(rows, 512)
    cos, sin = pl.pallas_call(
        _trig_kernel,
        grid=(rows // tr,),
        in_specs=[pl.BlockSpec((tr, lanes), lambda i: (i, 0)),
                  pl.BlockSpec((1, lanes), lambda i: (0, 0))],
        out_specs=[pl.BlockSpec((tr, lanes), lambda i: (i, 0))] * 2,
        out_shape=[jax.ShapeDtypeStruct((rows, lanes), F32)] * 2,
        compiler_params=_params("arbitrary"),
        name="rope_trig",
    )(pos_rep, inv)
    cos = cos.reshape(n, ROPE_HALF)
    sin = sin.reshape(n, ROPE_HALF)
    tail = HEAD_PAD - ROPE_LO - QK_ROPE
    ctab = jnp.concatenate([jnp.ones((n, ROPE_LO), F32), cos, cos, jnp.ones((n, tail), F32)], axis=1)
    stab = jnp.concatenate([jnp.zeros((n, ROPE_LO), F32), sin, sin, jnp.zeros((n, tail), F32)], axis=1)
    return ctab, stab


def _rope_slot(x, ctab, stab, lane):
    other = jnp.where(lane < ROPE_LO + ROPE_HALF,
                      -pltpu.roll(x, HEAD_PAD - ROPE_HALF, 1),
                      pltpu.roll(x, ROPE_HALF, 1))
    return x * ctab + other * stab


def _mla_proj_kernel(x_ref, mod_ref, w_in_ref, qg_ref, w_qb_ref, kvg_ref, w_kvb_ref,
                     ctab_ref, stab_ref, q_ref, k_ref, v_ref):
    h = _modulate(x_ref[...], mod_ref[0]).astype(BF16)
    proj = _dot(h, w_in_ref[...])
    qn = _rms(proj[:, :Q_LORA], qg_ref[...]).astype(BF16)
    kvn = _rms(proj[:, Q_LORA:Q_LORA + KV_LORA], kvg_ref[...]).astype(BF16)
    kr = proj[:, Q_LORA + KV_LORA:]
    q = _dot(qn, w_qb_ref[...])
    kv = _dot(kvn, w_kvb_ref[...])
    ctab = ctab_ref[...]
    stab = stab_ref[...]
    lane = lax.broadcasted_iota(jnp.int32, ctab.shape, 1)
    kr = _rope_slot(kr, ctab, stab, lane)
    scale = (QK_NOPE + QK_ROPE) ** -0.5 * LOG2_E
    for hd in range(MLA_HEADS):
        sl = slice(hd * HEAD_PAD, (hd + 1) * HEAD_PAD)
        q_ref[:, sl] = (_rope_slot(q[:, sl], ctab, stab, lane) * scale).astype(BF16)
        k_ref[:, sl] = (kv[:, sl] + kr).astype(BF16)
    v_ref[...] = kv[:, MLA_HEADS * HEAD_PAD:].astype(BF16)


def mla_proj(x, mod, w_in_p, qg, w_qb_p, kvg, w_kvb_p, ctab, stab, B, S, tm):
    N, D = x.shape
    nt = S // tm
    row = lambda b, i: (b * nt + i, 0)
    full = lambda b, i: (0, 0)
    HP = MLA_HEADS * HEAD_PAD
    HV = MLA_HEADS * V_HEAD
    return pl.pallas_call(
        _mla_proj_kernel,
        grid=(B, nt),
        in_specs=[pl.BlockSpec((tm, D), row),
                  pl.BlockSpec((1, 3, D), lambda b, i: (b, 0, 0)),
                  pl.BlockSpec(w_in_p.shape, full),
                  pl.BlockSpec(qg.shape, full),
                  pl.BlockSpec(w_qb_p.shape, full),
                  pl.BlockSpec(kvg.shape, full),
                  pl.BlockSpec(w_kvb_p.shape, full),
                  pl.BlockSpec((tm, HEAD_PAD), row),
                  pl.BlockSpec((tm, HEAD_PAD), row)],
        out_specs=[pl.BlockSpec((tm, HP), row),
                   pl.BlockSpec((tm, HP), row),
                   pl.BlockSpec((tm, HV), row)],
        out_shape=[jax.ShapeDtypeStruct((N, HP), BF16),
                   jax.ShapeDtypeStruct((N, HP), BF16),
                   jax.ShapeDtypeStruct((N, HV), BF16)],
        compiler_params=_params("arbitrary", "arbitrary"),
        name="mla_proj",
    )(x, mod, w_in_p, qg, w_qb_p, kvg, w_kvb_p, ctab, stab)


def _attn_kernel(q_ref, k_ref, v_ref, o_ref, *, tq, seq_len):
    causal = (lax.broadcasted_iota(jnp.int32, (tq, tq), 0)
              >= lax.broadcasted_iota(jnp.int32, (tq, tq), 1))
    first = lax.broadcasted_iota(jnp.int32, (tq, 2 * V_HEAD), 1) < V_HEAD

    def head_rows(q, kcols, q0):
        n = q0 + tq
        s_diag = jnp.where(causal, _dot_nt(q, k_ref[q0:n, kcols]), -1e30)
        m = jnp.max(s_diag, axis=-1, keepdims=True)
        if q0 > 0:
            s_prev = _dot_nt(q, k_ref[:q0, kcols])
            m = jnp.maximum(m, jnp.max(s_prev, axis=-1, keepdims=True))
        p_diag = jnp.exp2(s_diag - m)
        l = jnp.sum(p_diag, axis=-1, keepdims=True)
        o = _dot(p_diag.astype(BF16), v_ref[q0:n, :])
        if q0 > 0:
            p_prev = jnp.exp2(s_prev - m)
            l = l + jnp.sum(p_prev, axis=-1, keepdims=True)
            o = o + _dot(p_prev.astype(BF16), v_ref[:q0, :])
        return o / l

    for qi in range(seq_len // tq):
        q0 = qi * tq
        oa = head_rows(q_ref[q0:q0 + tq, :HEAD_PAD], slice(0, HEAD_PAD), q0)
        ob = head_rows(q_ref[q0:q0 + tq, HEAD_PAD:], slice(HEAD_PAD, 2 * HEAD_PAD), q0)
        o_ref[q0:q0 + tq, :] = jnp.where(first, oa, ob).astype(BF16)


def mla_attention(q, k, v, B, S, tq):
    N = q.shape[0]
    pairs = MLA_HEADS // 2
    HV = MLA_HEADS * V_HEAD
    blk = lambda b, hp: (b, hp)
    return pl.pallas_call(
        functools.partial(_attn_kernel, tq=tq, seq_len=S),
        grid=(B, pairs),
        in_specs=[pl.BlockSpec((S, 2 * HEAD_PAD), blk),
                  pl.BlockSpec((S, 2 * HEAD_PAD), blk),
                  pl.BlockSpec((S, 2 * V_HEAD), blk)],
        out_specs=pl.BlockSpec((S, 2 * V_HEAD), blk),
        out_shape=jax.ShapeDtypeStruct((N, HV), BF16),
        compiler_params=_params("arbitrary", "arbitrary"),
        name="mla_attention",
    )(q, k, v)


def _ffn_rows(x, mod, w_in_ref, w_out_ref, g, b, d_ff):
    h = _modulate(x, mod).astype(BF16)
    gate = _dot(h, w_in_ref[:, :d_ff])
    up = _dot(h, w_in_ref[:, d_ff:])
    act = (_silu(gate) * up).astype(BF16)
    return _deepnorm_ln(x, _dot(act, w_out_ref[...]), mod, g, b)


def _row_groups(rows):
    step = min(rows, LN_ROW_GROUP)
    return [slice(r0, r0 + step) for r0 in range(0, rows, step)]


def _ffn_kernel(x_ref, mod_ref, w_in_ref, w_out_ref, g_ref, b_ref, out_ref, *, d_ff):
    for r in _row_groups(x_ref.shape[0]):
        out_ref[r, :] = _ffn_rows(x_ref[r, :], mod_ref[0], w_in_ref, w_out_ref,
                                  g_ref[...], b_ref[...], d_ff)


def _proj_ffn_kernel(o_ref, x_ref, mod0_ref, w_o_ref, g0_ref, b0_ref,
                     mod_ref, w_in_ref, w_out_ref, g_ref, b_ref, out_ref, *, d_ff):
    for r in _row_groups(x_ref.shape[0]):
        x_mid = _deepnorm_ln(x_ref[r, :], _dot(o_ref[r, :], w_o_ref[...]), mod0_ref[0],
                             g0_ref[...], b0_ref[...])
        out_ref[r, :] = _ffn_rows(x_mid, mod_ref[0], w_in_ref, w_out_ref,
                                  g_ref[...], b_ref[...], d_ff)


def ffn(x, mod, w_in, w_out, g, b, B, S, tm, mixer_out=None):
    N, D = x.shape
    d_ff = w_out.shape[0]
    nt = S // tm
    row = lambda bb, i: (bb * nt + i, 0)
    full = lambda bb, i: (0, 0)
    per_batch = lambda bb, i: (bb, 0, 0)
    resident = functools.partial(pl.BlockSpec, index_map=full, pipeline_mode=pl.Buffered(1))
    vec = pl.BlockSpec((1, D), full)
    ffn_specs = [pl.BlockSpec((1, 3, D), per_batch), resident(w_in.shape), resident(w_out.shape),
                 vec, vec]
    if mixer_out is None:
        body = functools.partial(_ffn_kernel, d_ff=d_ff)
        in_specs = [pl.BlockSpec((tm, D), row)] + ffn_specs
        args = (x, mod, w_in, w_out, g, b)
    else:
        o, mod0, w_o, g0, b0 = mixer_out
        body = functools.partial(_proj_ffn_kernel, d_ff=d_ff)
        in_specs = [pl.BlockSpec((tm, o.shape[1]), row), pl.BlockSpec((tm, D), row),
                    pl.BlockSpec((1, 3, D), per_batch), resident(w_o.shape), vec, vec] + ffn_specs
        args = (o, x, mod0, w_o, g0, b0, mod, w_in, w_out, g, b)
    return pl.pallas_call(
        body,
        grid=(B, nt),
        in_specs=in_specs,
        out_specs=pl.BlockSpec((tm, D), row),
        out_shape=jax.ShapeDtypeStruct((N, D), F32),
        compiler_params=_params("arbitrary", "arbitrary"),
        name="ffn",
    )(*args)


def _split3(x):
    a = x.astype(BF16)
    r = x - a.astype(F32)
    b = r.astype(BF16)
    c = (r - b.astype(F32)).astype(BF16)
    return a, b, c


def _hgrn_kernel(x_ref, mod_ref, lb_ref, w_in_ref, gn_ref, w_o_ref, g_ref, b_ref, out_ref,
                 q_s, k_s, b_s, v_s, o_s, st_s, qt_s, kt_s, qe_s, kd_s, v16_s, dec_s, a_s, u_s, stb_s,
                 *, layer_idx, heads, tm):
    C = HGRN_CHUNK
    SUB = HGRN_SUB
    K = HGRN_EXPAND
    HK = heads * K
    n_chunks = tm // C

    @pl.when(pl.program_id(1) == 0)
    def _():
        st_s[...] = jnp.zeros(st_s.shape, F32)

    x = x_ref[...]
    mod = mod_ref[0]
    h = _modulate(x, mod).astype(BF16)

    lb_all = lb_ref[...]
    lb_max = jnp.max(lb_all, axis=0, keepdims=True)
    lb_exp = jnp.exp(lb_all - lb_max)
    lb_soft = lb_exp / jnp.sum(lb_exp, axis=0, keepdims=True)
    lb = jnp.sum(lb_soft[:layer_idx + 1], axis=0, keepdims=True) - lb_soft[0:1]

    q_all = _silu(_dot(h, w_in_ref[:, :HK]))
    f = lb + (1.0 - lb) * jax.nn.sigmoid(_dot(h, w_in_ref[:, HK:2 * HK]))
    logf = jnp.log(f) * LOG2_E
    k_all = 1.0 - f
    v_all = _dot(h, w_in_ref[:, 2 * HK:3 * HK])

    tri = (lax.broadcasted_iota(jnp.int32, (C, C), 0)
           >= lax.broadcasted_iota(jnp.int32, (C, C), 1)).astype(BF16)
    MID = C // 2
    half_span = jnp.zeros((1, HK), F32)
    for c in range(n_chunks):
        rows = slice(c * C, (c + 1) * C)
        g1, g2, g3 = _split3(logf[rows])
        bc = _dot(tri, g1) + _dot(tri, g2) + _dot(tri, g3)
        b_s[rows, :] = bc
        half_span = jnp.maximum(half_span, jnp.maximum(bc[0:1] - bc[MID:MID + 1],
                                                       bc[MID:MID + 1] - bc[C - 1:C]))
    q_s[...] = q_all
    k_s[...] = k_all
    v_s[...] = v_all
    factorable = jnp.max(half_span) <= MAX_LOG2_SPAN
    sls = [slice(hd * K, (hd + 1) * K) for hd in range(heads)]

    @pl.when(factorable)
    def _():
        causal = (lax.broadcasted_iota(jnp.int32, (C, C), 0)
                  >= lax.broadcasted_iota(jnp.int32, (C, C), 1))
        chunk_rows = [slice(c * C, (c + 1) * C) for c in range(n_chunks)]
        for c, rows in enumerate(chunk_rows):
            b = b_s[rows, :]
            q = q_s[rows, :]
            k = k_s[rows, :]
            ref = b[MID:MID + 1]
            b_last = b[C - 1:C]
            qt_s[rows, :] = (q * jnp.exp2(b - ref)).astype(BF16)
            kt_s[rows, :] = (k * jnp.exp2(ref - b)).astype(BF16)
            qe_s[rows, :] = (q * jnp.exp2(b)).astype(BF16)
            kd_s[rows, :] = (k * jnp.exp2(b_last - b)).astype(BF16)
            v16_s[rows, :] = v_s[rows, :].astype(BF16)
            dec_s[c:c + 1, :] = jnp.exp2(b_last)
        for c, rows in enumerate(chunk_rows):
            for hd, sl in enumerate(sls):
                a = jnp.where(causal, _dot_nt(qt_s[rows, sl], kt_s[rows, sl]), 0.0)
                a_s[hd, rows, :] = a.astype(BF16)
                u_s[c, hd] = _dot_tn(v16_s[rows, sl], kd_s[rows, sl])
        for hd, sl in enumerate(sls):
            st = st_s[hd]
            for c in range(n_chunks):
                stb_s[c, hd] = st.astype(BF16)
                st = st * dec_s[c:c + 1, sl] + u_s[c, hd]
            st_s[hd] = st
        for c, rows in enumerate(chunk_rows):
            for hd, sl in enumerate(sls):
                o_s[rows, sl] = (_dot(a_s[hd, rows, :], v16_s[rows, sl])
                                 + _dot_nt(qe_s[rows, sl], stb_s[c, hd]))

    grp_row = lax.broadcasted_iota(jnp.int32, (ROWS, K), 0)

    def chunk_head_direct(q, k, b, v, st):
        v16 = v.astype(BF16)
        b_last = b[C - 1:C]
        o_inter = _dot_nt((q * jnp.exp2(b)).astype(BF16), st.astype(BF16))
        k_dec = (k * jnp.exp2(b_last - b)).astype(BF16)
        st_new = st * jnp.exp2(b_last) + _dot_tn(v16, k_dec)
        outs = []
        for i in range(C // SUB):
            lo = i * SUB
            o_i = o_inter[lo:lo + SUB]
            if i > 0:
                ref = b[lo - 1:lo]
                q_t = (q[lo:lo + SUB] * jnp.exp2(b[lo:lo + SUB] - ref)).astype(BF16)
                k_t = (k[:lo] * jnp.exp2(ref - b[:lo])).astype(BF16)
                a = _dot_nt(q_t, k_t)
                o_i = o_i + _dot(a.astype(BF16), v16[:lo])
            for g in range(SUB // ROWS):
                glo = lo + g * ROWS
                bg = b[glo:glo + ROWS]
                qg = q[glo:glo + ROWS]
                o_g = o_i[g * ROWS:(g + 1) * ROWS]
                for s in range(lo, glo + ROWS):
                    e = jnp.exp2(bg - b[s:s + 1])
                    if s >= glo:
                        e = jnp.where(grp_row >= s - glo, e, 0.0)
                    w = jnp.sum(qg * e * k[s:s + 1], axis=-1, keepdims=True)
                    o_g = o_g + w * v[s:s + 1]
                outs.append(o_g)
        return jnp.concatenate(outs, axis=0), st_new

    @pl.when(jnp.logical_not(factorable))
    def _():
        def chunk(c, _):
            rows = pl.ds(pl.multiple_of(c * C, C), C)
            for hd, sl in enumerate(sls):
                o_h, st_new = chunk_head_direct(q_s[rows, sl], k_s[rows, sl], b_s[rows, sl],
                                                v_s[rows, sl], st_s[hd])
                st_s[hd] = st_new
                o_s[rows, sl] = o_h
            return 0

        lax.fori_loop(0, n_chunks, chunk, 0)

    gn = gn_ref[...]
    o = jnp.concatenate([_rms(o_s[:, hd * K:(hd + 1) * K], gn) for hd in range(heads)], axis=1)
    gate = _silu(_dot(h, w_in_ref[:, 3 * HK:]))
    _proj_ln_rows((o * gate).astype(BF16), x, mod, w_o_ref, g_ref[...], b_ref[...], out_ref)


def hgrn_layer(x, mod, lb_all, layer_idx, w_in, gn, w_o, g, b, B, S, tm):
    N, D = x.shape
    heads = D // HGRN_EXPAND
    nt = S // tm
    row = lambda bb, i: (bb * nt + i, 0)
    full = lambda bb, i: (0, 0)
    hs = pltpu.VMEM((tm, D), F32)
    hb = pltpu.VMEM((tm, D), BF16)
    n_chunks = tm // HGRN_CHUNK
    per_chunk_head = (n_chunks, heads, HGRN_EXPAND, HGRN_EXPAND)
    return pl.pallas_call(
        functools.partial(_hgrn_kernel, layer_idx=layer_idx, heads=heads, tm=tm),
        grid=(B, nt),
        in_specs=[pl.BlockSpec((tm, D), row),
                  pl.BlockSpec((1, 3, D), lambda bb, i: (bb, 0, 0)),
                  pl.BlockSpec(lb_all.shape, full),
                  pl.BlockSpec(w_in.shape, full, pipeline_mode=pl.Buffered(1)),
                  pl.BlockSpec((1, HGRN_EXPAND), full),
                  pl.BlockSpec(w_o.shape, full, pipeline_mode=pl.Buffered(1)),
                  pl.BlockSpec((1, D), full),
                  pl.BlockSpec((1, D), full)],
        out_specs=pl.BlockSpec((tm, D), row),
        out_shape=jax.ShapeDtypeStruct((N, D), F32),
        scratch_shapes=[hs, hs, hs, hs, hs,
                        pltpu.VMEM((heads, HGRN_EXPAND, HGRN_EXPAND), F32),
                        hb, hb, hb, hb, hb,
                        pltpu.VMEM((n_chunks, D), F32),
                        pltpu.VMEM((heads, tm, HGRN_CHUNK), BF16),
                        pltpu.VMEM(per_chunk_head, F32),
                        pltpu.VMEM(per_chunk_head, BF16)],
        compiler_params=_params("arbitrary", "arbitrary"),
        name="hgrn",
    )(x, mod, lb_all, w_in, gn, w_o, g, b)


def _mla_weights(w_in, w_qb, w_kvb):
    d = w_in.shape[0]
    rope_tail = HEAD_PAD - ROPE_LO - QK_ROPE
    w_in_p = jnp.concatenate(
        [w_in[:, :Q_LORA + KV_LORA], jnp.zeros((d, ROPE_LO), w_in.dtype),
         w_in[:, Q_LORA + KV_LORA:], jnp.zeros((d, rope_tail), w_in.dtype)], axis=1)
    qh = w_qb.reshape(Q_LORA, MLA_HEADS, QK_NOPE + QK_ROPE)
    w_qb_p = jnp.pad(qh, ((0, 0), (0, 0), (0, HEAD_PAD - QK_NOPE - QK_ROPE))).reshape(Q_LORA, -1)
    kvh = w_kvb.reshape(KV_LORA, MLA_HEADS, QK_NOPE + V_HEAD)
    w_k = jnp.pad(kvh[:, :, :QK_NOPE], ((0, 0), (0, 0), (0, HEAD_PAD - QK_NOPE))).reshape(KV_LORA, -1)
    w_v = kvh[:, :, QK_NOPE:].reshape(KV_LORA, -1)
    w_kvb_p = jnp.concatenate([w_k, w_v], axis=1)
    return w_in_p.astype(BF16), w_qb_p.astype(BF16), w_kvb_p.astype(BF16)


def kernel(x, c, positions, mla_w_in, mla_q_norm, mla_w_qb, mla_kv_norm, mla_w_kvb, mla_w_o,
           hgrn_lb, hgrn_w_in, hgrn_g_norm, hgrn_w_o, ffn_w_in, ffn_w_out,
           ada_w, ada_b, ln_g, ln_b):
    B, S, D = x.shape
    depth = ffn_w_in.shape[0]
    tm = min(S, 512)
    tq = min(S, 512)
    t_hgrn = min(S, 512)

    mods = ada_mods(c, ada_w, ada_b)
    ctab, stab = rope_tables(positions)
    xf = x.reshape(B * S, D)

    for layer in range(depth):
        j = layer // N_MIXERS
        mod = mods[2 * layer]
        g0 = ln_g[layer, 0].reshape(1, D)
        b0 = ln_b[layer, 0].reshape(1, D)
        mixer_out = None
        if layer % N_MIXERS == 0:
            w_in_p, w_qb_p, w_kvb_p = _mla_weights(mla_w_in[j], mla_w_qb[j], mla_w_kvb[j])
            q, k, v = mla_proj(xf, mod, w_in_p, mla_q_norm[j].reshape(1, -1), w_qb_p,
                               mla_kv_norm[j].reshape(1, -1), w_kvb_p, ctab, stab, B, S, tm)
            o = mla_attention(q, k, v, B, S, tq)
            mixer_out = (o, mod, mla_w_o[j].astype(BF16), g0, b0)
        else:
            xf = hgrn_layer(xf, mod, hgrn_lb, j, hgrn_w_in[j].astype(BF16),
                            hgrn_g_norm[j].reshape(1, -1), hgrn_w_o[j].astype(BF16),
                            g0, b0, B, S, t_hgrn)
        xf = ffn(xf, mods[2 * layer + 1], ffn_w_in[layer].astype(BF16),
                 ffn_w_out[layer].astype(BF16), ln_g[layer, 1].reshape(1, D),
                 ln_b[layer, 1].reshape(1, D), B, S, tm, mixer_out=mixer_out)
    return xf.reshape(B, S, D)
```

```python
import functools

import jax
import jax.numpy as jnp
from jax import lax
from jax.experimental import pallas as pl
from jax.experimental.pallas import tpu as pltpu

F32 = jnp.float32
BF16 = jnp.bfloat16

DEPTH = 4
N_MIXERS = 2

MLA_HEADS = 16
QK_NOPE = 64
QK_ROPE = 32
V_HEAD = 64
Q_LORA = 768
KV_LORA = 256
ROPE_THETA = 10000.0
HEAD_PAD = 128
ROPE_LO = QK_NOPE
ROPE_HALF = QK_ROPE // 2
ATTN_PAIRS_PER_STEP = 2

HGRN_EXPAND = 128
HGRN_CHUNK = 64
HGRN_SUB = 16
ROWS = 8
MAX_LOG2_SPAN = 100.0
LN_ROW_GROUP = 256

ALPHA = (2.0 * DEPTH) ** 0.25
LN_EPS = 1e-5
RMS_EPS = 1e-6
LOG2_E = 1.4426950408889634

VMEM_LIMIT_BYTES = 56 * 1024 * 1024


def _params(*semantics):
    return pltpu.CompilerParams(dimension_semantics=semantics,
                                vmem_limit_bytes=VMEM_LIMIT_BYTES)


def _dot(a, b):
    return jnp.dot(a, b, preferred_element_type=F32)


def _dot_nt(a, b):
    return lax.dot_general(a, b, (((1,), (1,)), ((), ())), preferred_element_type=F32)


def _dot_tn(a, b):
    return lax.dot_general(a, b, (((0,), (0,)), ((), ())), preferred_element_type=F32)


def _silu(x):
    return x * jax.nn.sigmoid(x)


def _modulate(x, mod):
    return x * (1.0 + mod[1:2]) + mod[0:1]


def _deepnorm_ln(x, y, mod, g, b):
    z = ALPHA * x + (1.0 + mod[2:3]) * y
    mu = jnp.mean(z, axis=-1, keepdims=True)
    zc = z - mu
    var = jnp.mean(zc * zc, axis=-1, keepdims=True)
    return zc * lax.rsqrt(var + LN_EPS) * g + b


def _rms(x, g):
    ms = jnp.mean(x * x, axis=-1, keepdims=True)
    return x * lax.rsqrt(ms + RMS_EPS) * g


def _proj_ln_rows(a, x, mod, w_ref, g, b, out_ref):
    rows = a.shape[0]
    step = min(rows, LN_ROW_GROUP)
    for r0 in range(0, rows, step):
        r = slice(r0, r0 + step)
        out_ref[r, :] = _deepnorm_ln(x[r], _dot(a[r], w_ref[...]), mod, g, b)


def _ada_kernel(c_ref, w_ref, b_ref, o_ref):
    sc = _silu(c_ref[...]).astype(BF16)
    o_ref[0] = _dot(sc, w_ref[0].astype(BF16)) + b_ref[0]


def ada_mods(c, ada_w, ada_b):
    B, D = c.shape
    L = ada_w.shape[0] * ada_w.shape[1]
    w = ada_w.reshape(L, D, 3 * D)
    b = ada_b.reshape(L, 1, 3 * D)
    out = pl.pallas_call(
        _ada_kernel,
        grid=(L, 3),
        in_specs=[pl.BlockSpec((B, D), lambda l, j: (0, 0)),
                  pl.BlockSpec((1, D, D), lambda l, j: (l, 0, j)),
                  pl.BlockSpec((1, 1, D), lambda l, j: (l, 0, j))],
        out_specs=pl.BlockSpec((1, B, D), lambda l, j: (l, 0, j)),
        out_shape=jax.ShapeDtypeStruct((L, B, 3 * D), F32),
        compiler_params=_params("arbitrary", "arbitrary"),
        name="ada_mods",
    )(c, w, b)
    return out.reshape(L, B, 3, D)


def _trig_kernel(pos_ref, inv_ref, cos_ref, sin_ref):
    ang = pos_ref[...].astype(F32) * inv_ref[...]
    cos_ref[...] = jnp.cos(ang)
    sin_ref[...] = jnp.sin(ang)


def rope_tables(positions):
    n = positions.size
    lanes = 128
    per_row = lanes // ROPE_HALF
    rows = n // per_row
    pos_rep = jnp.repeat(positions.reshape(-1), ROPE_HALF).reshape(rows, lanes)
    inv_freq = ROPE_THETA ** (-jnp.arange(0, QK_ROPE, 2, dtype=F32) / QK_ROPE)
    inv = jnp.tile(inv_freq, per_row).reshape(1, lanes)
    tr = min(rows, 512)
    cos, sin = pl.pallas_call(
        _trig_kernel,
        grid=(rows // tr,),
        in_specs=[pl.BlockSpec((tr, lanes), lambda i: (i, 0)),
                  pl.BlockSpec((1, lanes), lambda i: (0, 0))],
        out_specs=[pl.BlockSpec((tr, lanes), lambda i: (i, 0))] * 2,
        out_shape=[jax.ShapeDtypeStruct((rows, lanes), F32)] * 2,
        compiler_params=_params("arbitrary"),
        name="rope_trig",
    )(pos_rep, inv)
    cos = cos.reshape(n, ROPE_HALF)
    sin = sin.reshape(n, ROPE_HALF)
    tail = HEAD_PAD - ROPE_LO - QK_ROPE
    ctab = jnp.concatenate([jnp.ones((n, ROPE_LO), F32), cos, cos, jnp.ones((n, tail), F32)], axis=1)
    stab = jnp.concatenate([jnp.zeros((n, ROPE_LO), F32), sin, sin, jnp.zeros((n, tail), F32)], axis=1)
    return ctab, stab


def _rope_slot(x, ctab, stab, lane):
    other = jnp.where(lane < ROPE_LO + ROPE_HALF,
                      -pltpu.roll(x, HEAD_PAD - ROPE_HALF, 1),
                      pltpu.roll(x, ROPE_HALF, 1))
    return x * ctab + other * stab


def _mla_proj_kernel(x_ref, mod_ref, w_in_ref, qg_ref, w_qb_ref, kvg_ref, w_kvb_ref,
                     ctab_ref, stab_ref, q_ref, k_ref, v_ref):
    h = _modulate(x_ref[...], mod_ref[0]).astype(BF16)
    proj = _dot(h, w_in_ref[...])
    qn = _rms(proj[:, :Q_LORA], qg_ref[...]).astype(BF16)
    kvn = _rms(proj[:, Q_LORA:Q_LORA + KV_LORA], kvg_ref[...]).astype(BF16)
    kr = proj[:, Q_LORA + KV_LORA:]
    q = _dot(qn, w_qb_ref[...])
    kv = _dot(kvn, w_kvb_ref[...])
    ctab = ctab_ref[...]
    stab = stab_ref[...]
    lane = lax.broadcasted_iota(jnp.int32, ctab.shape, 1)
    kr = _rope_slot(kr, ctab, stab, lane)
    scale = (QK_NOPE + QK_ROPE) ** -0.5 * LOG2_E
    for hd in range(MLA_HEADS):
        sl = slice(hd * HEAD_PAD, (hd + 1) * HEAD_PAD)
        q_ref[:, sl] = (_rope_slot(q[:, sl], ctab, stab, lane) * scale).astype(BF16)
        k_ref[:, sl] = (kv[:, sl] + kr).astype(BF16)
    v_ref[...] = kv[:, MLA_HEADS * HEAD_PAD:].astype(BF16)


def mla_proj(x, mod, w_in_p, qg, w_qb_p, kvg, w_kvb_p, ctab, stab, B, S, tm):
    N, D = x.shape
    nt = S // tm
    row = lambda b, i: (b * nt + i, 0)
    full = lambda b, i: (0, 0)
    HP = MLA_HEADS * HEAD_PAD
    HV = MLA_HEADS * V_HEAD
    return pl.pallas_call(
        _mla_proj_kernel,
        grid=(B, nt),
        in_specs=[pl.BlockSpec((tm, D), row),
                  pl.BlockSpec((1, 3, D), lambda b, i: (b, 0, 0)),
                  pl.BlockSpec(w_in_p.shape, full),
                  pl.BlockSpec(qg.shape, full),
                  pl.BlockSpec(w_qb_p.shape, full),
                  pl.BlockSpec(kvg.shape, full),
                  pl.BlockSpec(w_kvb_p.shape, full),
                  pl.BlockSpec((tm, HEAD_PAD), row),
                  pl.BlockSpec((tm, HEAD_PAD), row)],
        out_specs=[pl.BlockSpec((tm, HP), row),
                   pl.BlockSpec((tm, HP), row),
                   pl.BlockSpec((tm, HV), row)],
        out_shape=[jax.ShapeDtypeStruct((N, HP), BF16),
                   jax.ShapeDtypeStruct((N, HP), BF16),
                   jax.ShapeDtypeStruct((N, HV), BF16)],
        compiler_params=_params("arbitrary", "arbitrary"),
        name="mla_proj",
    )(x, mod, w_in_p, qg, w_qb_p, kvg, w_kvb_p, ctab, stab)


def _attn_kernel(q_ref, k_ref, v_ref, o_ref, *, tq, seq_len, pairs):
    causal = (lax.broadcasted_iota(jnp.int32, (tq, tq), 0)
              >= lax.broadcasted_iota(jnp.int32, (tq, tq), 1))
    first = lax.broadcasted_iota(jnp.int32, (tq, 2 * V_HEAD), 1) < V_HEAD

    def head_rows(q0, hd):
        n = q0 + tq
        cols = slice(hd * HEAD_PAD, (hd + 1) * HEAD_PAD)
        vcols = slice((hd // 2) * 2 * V_HEAD, (hd // 2 + 1) * 2 * V_HEAD)
        q = q_ref[q0:n, cols]
        s_diag = jnp.where(causal, _dot_nt(q, k_ref[q0:n, cols]), -1e30)
        m = jnp.max(s_diag, axis=-1, keepdims=True)
        if q0 > 0:
            s_prev = _dot_nt(q, k_ref[:q0, cols])
            m = jnp.maximum(m, jnp.max(s_prev, axis=-1, keepdims=True))
        p_diag = jnp.exp2(s_diag - m)
        l = jnp.sum(p_diag, axis=-1, keepdims=True)
        o = _dot(p_diag.astype(BF16), v_ref[q0:n, vcols])
        if q0 > 0:
            p_prev = jnp.exp2(s_prev - m)
            l = l + jnp.sum(p_prev, axis=-1, keepdims=True)
            o = o + _dot(p_prev.astype(BF16), v_ref[:q0, vcols])
        return o / l

    for qi in reversed(range(seq_len // tq)):
        q0 = qi * tq
        for hp in range(pairs):
            o = jnp.where(first, head_rows(q0, 2 * hp), head_rows(q0, 2 * hp + 1))
            o_ref[q0:q0 + tq, hp * 2 * V_HEAD:(hp + 1) * 2 * V_HEAD] = o.astype(BF16)


def mla_attention(q, k, v, B, S, tq):
    N = q.shape[0]
    pairs = ATTN_PAIRS_PER_STEP
    HV = MLA_HEADS * V_HEAD
    blk = lambda b, g: (b, g)
    return pl.pallas_call(
        functools.partial(_attn_kernel, tq=tq, seq_len=S, pairs=pairs),
        grid=(B, MLA_HEADS // (2 * pairs)),
        in_specs=[pl.BlockSpec((S, pairs * 2 * HEAD_PAD), blk),
                  pl.BlockSpec((S, pairs * 2 * HEAD_PAD), blk),
                  pl.BlockSpec((S, pairs * 2 * V_HEAD), blk)],
        out_specs=pl.BlockSpec((S, pairs * 2 * V_HEAD), blk),
        out_shape=jax.ShapeDtypeStruct((N, HV), BF16),
        compiler_params=_params("arbitrary", "arbitrary"),
        name="mla_attention",
    )(q, k, v)


def _ffn_rows(x, mod, w_in_ref, w_out_ref, g, b, d_ff):
    h = _modulate(x, mod).astype(BF16)
    gate = _dot(h, w_in_ref[:, :d_ff])
    up = _dot(h, w_in_ref[:, d_ff:])
    act = (_silu(gate) * up).astype(BF16)
    return _deepnorm_ln(x, _dot(act, w_out_ref[...]), mod, g, b)


def _row_groups(rows):
    step = min(rows, LN_ROW_GROUP)
    return [slice(r0, r0 + step) for r0 in range(0, rows, step)]


def _ffn_kernel(x_ref, mod_ref, w_in_ref, w_out_ref, g_ref, b_ref, out_ref, *, d_ff):
    for r in _row_groups(x_ref.shape[0]):
        out_ref[r, :] = _ffn_rows(x_ref[r, :], mod_ref[0], w_in_ref, w_out_ref,
                                  g_ref[...], b_ref[...], d_ff)


def _proj_ffn_kernel(o_ref, x_ref, mod0_ref, w_o_ref, g0_ref, b0_ref,
                     mod_ref, w_in_ref, w_out_ref, g_ref, b_ref, out_ref, *, d_ff):
    for r in _row_groups(x_ref.shape[0]):
        x_mid = _deepnorm_ln(x_ref[r, :], _dot(o_ref[r, :], w_o_ref[...]), mod0_ref[0],
                             g0_ref[...], b0_ref[...])
        out_ref[r, :] = _ffn_rows(x_mid, mod_ref[0], w_in_ref, w_out_ref,
                                  g_ref[...], b_ref[...], d_ff)


def ffn(x, mod, w_in, w_out, g, b, B, S, tm, mixer_out=None):
    N, D = x.shape
    d_ff = w_out.shape[0]
    nt = S // tm
    row = lambda bb, i: (bb * nt + i, 0)
    full = lambda bb, i: (0, 0)
    per_batch = lambda bb, i: (bb, 0, 0)
    resident = functools.partial(pl.BlockSpec, index_map=full, pipeline_mode=pl.Buffered(1))
    vec = pl.BlockSpec((1, D), full)
    ffn_specs = [pl.BlockSpec((1, 3, D), per_batch), resident(w_in.shape), resident(w_out.shape),
                 vec, vec]
    if mixer_out is None:
        body = functools.partial(_ffn_kernel, d_ff=d_ff)
        in_specs = [pl.BlockSpec((tm, D), row)] + ffn_specs
        args = (x, mod, w_in, w_out, g, b)
    else:
        o, mod0, w_o, g0, b0 = mixer_out
        body = functools.partial(_proj_ffn_kernel, d_ff=d_ff)
        in_specs = [pl.BlockSpec((tm, o.shape[1]), row), pl.BlockSpec((tm, D), row),
                    pl.BlockSpec((1, 3, D), per_batch), resident(w_o.shape), vec, vec] + ffn_specs
        args = (o, x, mod0, w_o, g0, b0, mod, w_in, w_out, g, b)
    return pl.pallas_call(
        body,
        grid=(B, nt),
        in_specs=in_specs,
        out_specs=pl.BlockSpec((tm, D), row),
        out_shape=jax.ShapeDtypeStruct((N, D), F32),
        compiler_params=_params("arbitrary", "arbitrary"),
        name="ffn",
    )(*args)


def _split3(x):
    a = x.astype(BF16)
    r = x - a.astype(F32)
    b = r.astype(BF16)
    c = (r - b.astype(F32)).astype(BF16)
    return a, b, c


def _hgrn_kernel(x_ref, mod_ref, lb_ref, w_in_ref, gn_ref, w_o_ref, g_ref, b_ref, out_ref,
                 q_s, k_s, b_s, v_s, o_s, st_s, qt_s, kt_s, qe_s, kd_s, v16_s, dec_s, a_s, u_s, stb_s,
                 *, layer_idx, heads, tm):
    C = HGRN_CHUNK
    SUB = HGRN_SUB
    K = HGRN_EXPAND
    HK = heads * K
    n_chunks = tm // C

    @pl.when(pl.program_id(1) == 0)
    def _():
        st_s[...] = jnp.zeros(st_s.shape, F32)

    x = x_ref[...]
    mod = mod_ref[0]
    h = _modulate(x, mod).astype(BF16)

    lb_all = lb_ref[...]
    lb_max = jnp.max(lb_all, axis=0, keepdims=True)
    lb_exp = jnp.exp(lb_all - lb_max)
    lb_soft = lb_exp / jnp.sum(lb_exp, axis=0, keepdims=True)
    lb = jnp.sum(lb_soft[:layer_idx + 1], axis=0, keepdims=True) - lb_soft[0:1]

    q_all = _silu(_dot(h, w_in_ref[:, :HK]))
    f = lb + (1.0 - lb) * jax.nn.sigmoid(_dot(h, w_in_ref[:, HK:2 * HK]))
    logf = jnp.log(f) * LOG2_E
    k_all = 1.0 - f
    v_all = _dot(h, w_in_ref[:, 2 * HK:3 * HK])

    tri = (lax.broadcasted_iota(jnp.int32, (C, C), 0)
           >= lax.broadcasted_iota(jnp.int32, (C, C), 1)).astype(BF16)
    MID = C // 2
    half_span = jnp.zeros((1, HK), F32)
    for c in range(n_chunks):
        rows = slice(c * C, (c + 1) * C)
        g1, g2, g3 = _split3(logf[rows])
        bc = _dot(tri, g1) + _dot(tri, g2) + _dot(tri, g3)
        b_s[rows, :] = bc
        half_span = jnp.maximum(half_span, jnp.maximum(bc[0:1] - bc[MID:MID + 1],
                                                       bc[MID:MID + 1] - bc[C - 1:C]))
    q_s[...] = q_all
    k_s[...] = k_all
    v_s[...] = v_all
    factorable = jnp.max(half_span) <= MAX_LOG2_SPAN
    sls = [slice(hd * K, (hd + 1) * K) for hd in range(heads)]

    @pl.when(factorable)
    def _():
        causal = (lax.broadcasted_iota(jnp.int32, (C, C), 0)
                  >= lax.broadcasted_iota(jnp.int32, (C, C), 1))
        chunk_rows = [slice(c * C, (c + 1) * C) for c in range(n_chunks)]
        for c, rows in enumerate(chunk_rows):
            b = b_s[rows, :]
            q = q_s[rows, :]
            k = k_s[rows, :]
            ref = b[MID:MID + 1]
            b_last = b[C - 1:C]
            qt_s[rows, :] = (q * jnp.exp2(b - ref)).astype(BF16)
            kt_s[rows, :] = (k * jnp.exp2(ref - b)).astype(BF16)
            qe_s[rows, :] = (q * jnp.exp2(b)).astype(BF16)
            kd_s[rows, :] = (k * jnp.exp2(b_last - b)).astype(BF16)
            v16_s[rows, :] = v_s[rows, :].astype(BF16)
            dec_s[c:c + 1, :] = jnp.exp2(b_last)
        for c, rows in enumerate(chunk_rows):
            for hd, sl in enumerate(sls):
                a = jnp.where(causal, _dot_nt(qt_s[rows, sl], kt_s[rows, sl]), 0.0)
                a_s[hd, rows, :] = a.astype(BF16)
                u_s[c, hd] = _dot_tn(v16_s[rows, sl], kd_s[rows, sl])
        for hd, sl in enumerate(sls):
            st = st_s[hd]
            for c in range(n_chunks):
                stb_s[c, hd] = st.astype(BF16)
                st = st * dec_s[c:c + 1, sl] + u_s[c, hd]
            st_s[hd] = st
        for c, rows in enumerate(chunk_rows):
            for hd, sl in enumerate(sls):
                o_s[rows, sl] = (_dot(a_s[hd, rows, :], v16_s[rows, sl])
                                 + _dot_nt(qe_s[rows, sl], stb_s[c, hd]))

    grp_row = lax.broadcasted_iota(jnp.int32, (ROWS, K), 0)

    def chunk_head_direct(q, k, b, v, st):
        v16 = v.astype(BF16)
        b_last = b[C - 1:C]
        o_inter = _dot_nt((q * jnp.exp2(b)).astype(BF16), st.astype(BF16))
        k_dec = (k * jnp.exp2(b_last - b)).astype(BF16)
        st_new = st * jnp.exp2(b_last) + _dot_tn(v16, k_dec)
        outs = []
        for i in range(C // SUB):
            lo = i * SUB
            o_i = o_inter[lo:lo + SUB]
            if i > 0:
                ref = b[lo - 1:lo]
                q_t = (q[lo:lo + SUB] * jnp.exp2(b[lo:lo + SUB] - ref)).astype(BF16)
                k_t = (k[:lo] * jnp.exp2(ref - b[:lo])).astype(BF16)
                a = _dot_nt(q_t, k_t)
                o_i = o_i + _dot(a.astype(BF16), v16[:lo])
            for g in range(SUB // ROWS):
                glo = lo + g * ROWS
                bg = b[glo:glo + ROWS]
                qg = q[glo:glo + ROWS]
                o_g = o_i[g * ROWS:(g + 1) * ROWS]
                for s in range(lo, glo + ROWS):
                    e = jnp.exp2(bg - b[s:s + 1])
                    if s >= glo:
                        e = jnp.where(grp_row >= s - glo, e, 0.0)
                    w = jnp.sum(qg * e * k[s:s + 1], axis=-1, keepdims=True)
                    o_g = o_g + w * v[s:s + 1]
                outs.append(o_g)
        return jnp.concatenate(outs, axis=0), st_new

    @pl.when(jnp.logical_not(factorable))
    def _():
        def chunk(c, _):
            rows = pl.ds(pl.multiple_of(c * C, C), C)
            for hd, sl in enumerate(sls):
                o_h, st_new = chunk_head_direct(q_s[rows, sl], k_s[rows, sl], b_s[rows, sl],
                                                v_s[rows, sl], st_s[hd])
                st_s[hd] = st_new
                o_s[rows, sl] = o_h
            return 0

        lax.fori_loop(0, n_chunks, chunk, 0)

    gn = gn_ref[...]
    o = jnp.concatenate([_rms(o_s[:, hd * K:(hd + 1) * K], gn) for hd in range(heads)], axis=1)
    gate = _silu(_dot(h, w_in_ref[:, 3 * HK:]))
    _proj_ln_rows((o * gate).astype(BF16), x, mod, w_o_ref, g_ref[...], b_ref[...], out_ref)


def hgrn_layer(x, mod, lb_all, layer_idx, w_in, gn, w_o, g, b, B, S, tm):
    N, D = x.shape
    heads = D // HGRN_EXPAND
    nt = S // tm
    row = lambda bb, i: (bb * nt + i, 0)
    full = lambda bb, i: (0, 0)
    hs = pltpu.VMEM((tm, D), F32)
    hb = pltpu.VMEM((tm, D), BF16)
    n_chunks = tm // HGRN_CHUNK
    per_chunk_head = (n_chunks, heads, HGRN_EXPAND, HGRN_EXPAND)
    return pl.pallas_call(
        functools.partial(_hgrn_kernel, layer_idx=layer_idx, heads=heads, tm=tm),
        grid=(B, nt),
        in_specs=[pl.BlockSpec((tm, D), row),
                  pl.BlockSpec((1, 3, D), lambda bb, i: (bb, 0, 0)),
                  pl.BlockSpec(lb_all.shape, full),
                  pl.BlockSpec(w_in.shape, full, pipeline_mode=pl.Buffered(1)),
                  pl.BlockSpec((1, HGRN_EXPAND), full),
                  pl.BlockSpec(w_o.shape, full, pipeline_mode=pl.Buffered(1)),
                  pl.BlockSpec((1, D), full),
                  pl.BlockSpec((1, D), full)],
        out_specs=pl.BlockSpec((tm, D), row),
        out_shape=jax.ShapeDtypeStruct((N, D), F32),
        scratch_shapes=[hs, hs, hs, hs, hs,
                        pltpu.VMEM((heads, HGRN_EXPAND, HGRN_EXPAND), F32),
                        hb, hb, hb, hb, hb,
                        pltpu.VMEM((n_chunks, D), F32),
                        pltpu.VMEM((heads, tm, HGRN_CHUNK), BF16),
                        pltpu.VMEM(per_chunk_head, F32),
                        pltpu.VMEM(per_chunk_head, BF16)],
        compiler_params=_params("arbitrary", "arbitrary"),
        name="hgrn",
    )(x, mod, lb_all, w_in, gn, w_o, g, b)


def _mla_weights(w_in, w_qb, w_kvb):
    d = w_in.shape[0]
    rope_tail = HEAD_PAD - ROPE_LO - QK_ROPE
    w_in_p = jnp.concatenate(
        [w_in[:, :Q_LORA + KV_LORA], jnp.zeros((d, ROPE_LO), w_in.dtype),
         w_in[:, Q_LORA + KV_LORA:], jnp.zeros((d, rope_tail), w_in.dtype)], axis=1)
    qh = w_qb.reshape(Q_LORA, MLA_HEADS, QK_NOPE + QK_ROPE)
    w_qb_p = jnp.pad(qh, ((0, 0), (0, 0), (0, HEAD_PAD - QK_NOPE - QK_ROPE))).reshape(Q_LORA, -1)
    kvh = w_kvb.reshape(KV_LORA, MLA_HEADS, QK_NOPE + V_HEAD)
    w_k = jnp.pad(kvh[:, :, :QK_NOPE], ((0, 0), (0, 0), (0, HEAD_PAD - QK_NOPE))).reshape(KV_LORA, -1)
    w_v = kvh[:, :, QK_NOPE:].reshape(KV_LORA, -1)
    w_kvb_p = jnp.concatenate([w_k, w_v], axis=1)
    return w_in_p.astype(BF16), w_qb_p.astype(BF16), w_kvb_p.astype(BF16)


def kernel(x, c, positions, mla_w_in, mla_q_norm, mla_w_qb, mla_kv_norm, mla_w_kvb, mla_w_o,
           hgrn_lb, hgrn_w_in, hgrn_g_norm, hgrn_w_o, ffn_w_in, ffn_w_out,
           ada_w, ada_b, ln_g, ln_b):
    B, S, D = x.shape
    depth = ffn_w_in.shape[0]
    tm = min(S, 512)
    tq = min(S, 512)
    t_hgrn = min(S, 512)

    mods = ada_mods(c, ada_w, ada_b)
    ctab, stab = rope_tables(positions)
    xf = x.reshape(B * S, D)

    for layer in range(depth):
        j = layer // N_MIXERS
        mod = mods[2 * layer]
        g0 = ln_g[layer, 0].reshape(1, D)
        b0 = ln_b[layer, 0].reshape(1, D)
        mixer_out = None
        if layer % N_MIXERS == 0:
            w_in_p, w_qb_p, w_kvb_p = _mla_weights(mla_w_in[j], mla_w_qb[j], mla_w_kvb[j])
            q, k, v = mla_proj(xf, mod, w_in_p, mla_q_norm[j].reshape(1, -1), w_qb_p,
                               mla_kv_norm[j].reshape(1, -1), w_kvb_p, ctab, stab, B, S, tm)
            o = mla_attention(q, k, v, B, S, tq)
            mixer_out = (o, mod, mla_w_o[j].astype(BF16), g0, b0)
        else:
            xf = hgrn_layer(xf, mod, hgrn_lb, j, hgrn_w_in[j].astype(BF16),
                            hgrn_g_norm[j].reshape(1, -1), hgrn_w_o[j].astype(BF16),
                            g0, b0, B, S, t_hgrn)
        xf = ffn(xf, mods[2 * layer + 1], ffn_w_in[layer].astype(BF16),
                 ffn_w_out[layer].astype(BF16), ln_g[layer, 1].reshape(1, D),
                 ln_b[layer, 1].reshape(1, D), B, S, tm, mixer_out=mixer_out)
    return xf.reshape(B, S, D)
```

```python
import functools

import jax
import jax.numpy as jnp
from jax import lax
from jax.experimental import pallas as pl
from jax.experimental.pallas import tpu as pltpu

F32 = jnp.float32
BF16 = jnp.bfloat16

DEPTH = 4
N_MIXERS = 2

MLA_HEADS = 16
QK_NOPE = 64
QK_ROPE = 32
V_HEAD = 64
Q_LORA = 768
KV_LORA = 256
ROPE_THETA = 10000.0
HEAD_PAD = 128
ROPE_LO = QK_NOPE
ROPE_HALF = QK_ROPE // 2
ATTN_PAIRS_PER_STEP = 2

HGRN_EXPAND = 128
HGRN_CHUNK = 64
HGRN_SUB = 16
ROWS = 8
MAX_LOG2_SPAN = 100.0
LN_ROW_GROUP = 256

ALPHA = (2.0 * DEPTH) ** 0.25
LN_EPS = 1e-5
RMS_EPS = 1e-6
LOG2_E = 1.4426950408889634

VMEM_LIMIT_BYTES = 56 * 1024 * 1024


def _params(*semantics):
    return pltpu.CompilerParams(dimension_semantics=semantics,
                                vmem_limit_bytes=VMEM_LIMIT_BYTES)


def _dot(a, b):
    return jnp.dot(a, b, preferred_element_type=F32)


def _dot_nt(a, b):
    return lax.dot_general(a, b, (((1,), (1,)), ((), ())), preferred_element_type=F32)


def _dot_tn(a, b):
    return lax.dot_general(a, b, (((0,), (0,)), ((), ())), preferred_element_type=F32)


def _silu(x):
    return x * jax.nn.sigmoid(x)


def _modulate(x, mod):
    return x * (1.0 + mod[1:2]) + mod[0:1]


def _deepnorm_ln(x, y, mod, g, b):
    z = ALPHA * x + (1.0 + mod[2:3]) * y
    mu = jnp.mean(z, axis=-1, keepdims=True)
    zc = z - mu
    var = jnp.mean(zc * zc, axis=-1, keepdims=True)
    return zc * lax.rsqrt(var + LN_EPS) * g + b


def _rms(x, g):
    ms = jnp.mean(x * x, axis=-1, keepdims=True)
    return x * lax.rsqrt(ms + RMS_EPS) * g


def _proj_ln_rows(a, x, mod, w_ref, g, b, out_ref):
    rows = a.shape[0]
    step = min(rows, LN_ROW_GROUP)
    for r0 in range(0, rows, step):
        r = slice(r0, r0 + step)
        out_ref[r, :] = _deepnorm_ln(x[r], _dot(a[r], w_ref[...]), mod, g, b)


def _ada_kernel(c_ref, w_ref, b_ref, o_ref):
    sc = _silu(c_ref[...]).astype(BF16)
    o_ref[0] = _dot(sc, w_ref[0].astype(BF16)) + b_ref[0]


def ada_mods(c, ada_w, ada_b):
    B, D = c.shape
    L = ada_w.shape[0] * ada_w.shape[1]
    w = ada_w.reshape(L, D, 3 * D)
    b = ada_b.reshape(L, 1, 3 * D)
    out = pl.pallas_call(
        _ada_kernel,
        grid=(L, 3),
        in_specs=[pl.BlockSpec((B, D), lambda l, j: (0, 0)),
                  pl.BlockSpec((1, D, D), lambda l, j: (l, 0, j)),
                  pl.BlockSpec((1, 1, D), lambda l, j: (l, 0, j))],
        out_specs=pl.BlockSpec((1, B, D), lambda l, j: (l, 0, j)),
        out_shape=jax.ShapeDtypeStruct((L, B, 3 * D), F32),
        compiler_params=_params("arbitrary", "arbitrary"),
        name="ada_mods",
    )(c, w, b)
    return out.reshape(L, B, 3, D)


def _trig_kernel(pos_ref, inv_ref, cos_ref, sin_ref):
    ang = pos_ref[...].astype(F32) * inv_ref[...]
    cos_ref[...] = jnp.cos(ang)
    sin_ref[...] = jnp.sin(ang)


def rope_tables(positions):
    n = positions.size
    lanes = 128
    per_row = lanes // ROPE_HALF
    rows = n // per_row
    pos_rep = jnp.repeat(positions.reshape(-1), ROPE_HALF).reshape(rows, lanes)
    inv_freq = ROPE_THETA ** (-jnp.arange(0, QK_ROPE, 2, dtype=F32) / QK_ROPE)
    inv = jnp.tile(inv_freq, per_row).reshape(1, lanes)
    tr = min(rows, 512)
    cos, sin = pl.pallas_call(
        _trig_kernel,
        grid=(rows // tr,),
        in_specs=[pl.BlockSpec((tr, lanes), lambda i: (i, 0)),
                  pl.BlockSpec((1, lanes), lambda i: (0, 0))],
        out_specs=[pl.BlockSpec((tr, lanes), lambda i: (i, 0))] * 2,
        out_shape=[jax.ShapeDtypeStruct((rows, lanes), F32)] * 2,
        compiler_params=_params("arbitrary"),
        name="rope_trig",
    )(pos_rep, inv)
    cos = cos.reshape(n, ROPE_HALF)
    sin = sin.reshape(n, ROPE_HALF)
    tail = HEAD_PAD - ROPE_LO - QK_ROPE
    ctab = jnp.concatenate([jnp.ones((n, ROPE_LO), F32), cos, cos, jnp.ones((n, tail), F32)], axis=1)
    stab = jnp.concatenate([jnp.zeros((n, ROPE_LO), F32), sin, sin, jnp.zeros((n, tail), F32)], axis=1)
    return ctab, stab


def _rope_slot(x, ctab, stab, lane):
    other = jnp.where(lane < ROPE_LO + ROPE_HALF,
                      -pltpu.roll(x, HEAD_PAD - ROPE_HALF, 1),
                      pltpu.roll(x, ROPE_HALF, 1))
    return x * ctab + other * stab


def _mla_proj_kernel(x_ref, mod_ref, w_in_ref, qg_ref, w_qb_ref, kvg_ref, w_kvb_ref,
                     ctab_ref, stab_ref, q_ref, k_ref, v_ref):
    h = _modulate(x_ref[...], mod_ref[0]).astype(BF16)
    proj = _dot(h, w_in_ref[...])
    qn = _rms(proj[:, :Q_LORA], qg_ref[...]).astype(BF16)
    kvn = _rms(proj[:, Q_LORA:Q_LORA + KV_LORA], kvg_ref[...]).astype(BF16)
    kr = proj[:, Q_LORA + KV_LORA:]
    q = _dot(qn, w_qb_ref[...])
    kv = _dot(kvn, w_kvb_ref[...])
    ctab = ctab_ref[...]
    stab = stab_ref[...]
    lane = lax.broadcasted_iota(jnp.int32, ctab.shape, 1)
    kr = _rope_slot(kr, ctab, stab, lane)
    scale = (QK_NOPE + QK_ROPE) ** -0.5 * LOG2_E
    for hd in range(MLA_HEADS):
        sl = slice(hd * HEAD_PAD, (hd + 1) * HEAD_PAD)
        q_ref[:, sl] = (_rope_slot(q[:, sl], ctab, stab, lane) * scale).astype(BF16)
        k_ref[:, sl] = (kv[:, sl] + kr).astype(BF16)
    v_ref[...] = kv[:, MLA_HEADS * HEAD_PAD:].astype(BF16)


def mla_proj(x, mod, w_in_p, qg, w_qb_p, kvg, w_kvb_p, ctab, stab, B, S, tm):
    N, D = x.shape
    nt = S // tm
    row = lambda b, i: (b * nt + i, 0)
    full = lambda b, i: (0, 0)
    HP = MLA_HEADS * HEAD_PAD
    HV = MLA_HEADS * V_HEAD
    return pl.pallas_call(
        _mla_proj_kernel,
        grid=(B, nt),
        in_specs=[pl.BlockSpec((tm, D), row),
                  pl.BlockSpec((1, 3, D), lambda b, i: (b, 0, 0)),
                  pl.BlockSpec(w_in_p.shape, full),
                  pl.BlockSpec(qg.shape, full),
                  pl.BlockSpec(w_qb_p.shape, full),
                  pl.BlockSpec(kvg.shape, full),
                  pl.BlockSpec(w_kvb_p.shape, full),
                  pl.BlockSpec((tm, HEAD_PAD), row),
                  pl.BlockSpec((tm, HEAD_PAD), row)],
        out_specs=[pl.BlockSpec((tm, HP), row),
                   pl.BlockSpec((tm, HP), row),
                   pl.BlockSpec((tm, HV), row)],
        out_shape=[jax.ShapeDtypeStruct((N, HP), BF16),
                   jax.ShapeDtypeStruct((N, HP), BF16),
                   jax.ShapeDtypeStruct((N, HV), BF16)],
        compiler_params=_params("arbitrary", "arbitrary"),
        name="mla_proj",
    )(x, mod, w_in_p, qg, w_qb_p, kvg, w_kvb_p, ctab, stab)


def _attn_kernel(q_ref, k_ref, v_ref, o_ref, *, tq, seq_len, pairs):
    causal = (lax.broadcasted_iota(jnp.int32, (tq, tq), 0)
              >= lax.broadcasted_iota(jnp.int32, (tq, tq), 1))
    first = lax.broadcasted_iota(jnp.int32, (tq, 2 * V_HEAD), 1) < V_HEAD

    def head_rows(q0, hd):
        n = q0 + tq
        cols = slice(hd * HEAD_PAD, (hd + 1) * HEAD_PAD)
        vcols = slice((hd // 2) * 2 * V_HEAD, (hd // 2 + 1) * 2 * V_HEAD)
        q = q_ref[q0:n, cols]
        s_diag = jnp.where(causal, _dot_nt(q, k_ref[q0:n, cols]), -1e30)
        m = jnp.max(s_diag, axis=-1, keepdims=True)
        if q0 > 0:
            s_prev = _dot_nt(q, k_ref[:q0, cols])
            m = jnp.maximum(m, jnp.max(s_prev, axis=-1, keepdims=True))
        p_diag = jnp.exp2(s_diag - m)
        l = jnp.sum(p_diag, axis=-1, keepdims=True)
        o = _dot(p_diag.astype(BF16), v_ref[q0:n, vcols])
        if q0 > 0:
            p_prev = jnp.exp2(s_prev - m)
            l = l + jnp.sum(p_prev, axis=-1, keepdims=True)
            o = o + _dot(p_prev.astype(BF16), v_ref[:q0, vcols])
        return o / l

    for qi in reversed(range(seq_len // tq)):
        q0 = qi * tq
        for hp in range(pairs):
            o = jnp.where(first, head_rows(q0, 2 * hp), head_rows(q0, 2 * hp + 1))
            o_ref[q0:q0 + tq, hp * 2 * V_HEAD:(hp + 1) * 2 * V_HEAD] = o.astype(BF16)


def mla_attention(q, k, v, B, S, tq):
    N = q.shape[0]
    pairs = ATTN_PAIRS_PER_STEP
    HV = MLA_HEADS * V_HEAD
    blk = lambda b, g: (b, g)
    return pl.pallas_call(
        functools.partial(_attn_kernel, tq=tq, seq_len=S, pairs=pairs),
        grid=(B, MLA_HEADS // (2 * pairs)),
        in_specs=[pl.BlockSpec((S, pairs * 2 * HEAD_PAD), blk),
                  pl.BlockSpec((S, pairs * 2 * HEAD_PAD), blk),
                  pl.BlockSpec((S, pairs * 2 * V_HEAD), blk)],
        out_specs=pl.BlockSpec((S, pairs * 2 * V_HEAD), blk),
        out_shape=jax.ShapeDtypeStruct((N, HV), BF16),
        compiler_params=_params("arbitrary", "arbitrary"),
        name="mla_attention",
    )(q, k, v)


def _ffn_rows(x, mod, w_in_ref, w_out_ref, g, b, d_ff):
    h = _modulate(x, mod).astype(BF16)
    gate = _dot(h, w_in_ref[:, :d_ff])
    up = _dot(h, w_in_ref[:, d_ff:])
    act = (_silu(gate) * up).astype(BF16)
    return _deepnorm_ln(x, _dot(act, w_out_ref[...]), mod, g, b)


def _row_groups(rows):
    step = min(rows, LN_ROW_GROUP)
    return [slice(r0, r0 + step) for r0 in range(0, rows, step)]


def _ffn_kernel(x_ref, mod_ref, w_in_ref, w_out_ref, g_ref, b_ref, out_ref, *, d_ff):
    for r in _row_groups(x_ref.shape[0]):
        out_ref[r, :] = _ffn_rows(x_ref[r, :], mod_ref[0], w_in_ref, w_out_ref,
                                  g_ref[...], b_ref[...], d_ff)


def _proj_ffn_kernel(o_ref, x_ref, mod0_ref, w_o_ref, g0_ref, b0_ref,
                     mod_ref, w_in_ref, w_out_ref, g_ref, b_ref, out_ref, *, d_ff):
    for r in _row_groups(x_ref.shape[0]):
        x_mid = _deepnorm_ln(x_ref[r, :], _dot(o_ref[r, :], w_o_ref[...]), mod0_ref[0],
                             g0_ref[...], b0_ref[...])
        out_ref[r, :] = _ffn_rows(x_mid, mod_ref[0], w_in_ref, w_out_ref,
                                  g_ref[...], b_ref[...], d_ff)


def ffn(x, mod, w_in, w_out, g, b, B, S, tm, mixer_out=None):
    N, D = x.shape
    d_ff = w_out.shape[0]
    nt = S // tm
    row = lambda bb, i: (bb * nt + i, 0)
    full = lambda bb, i: (0, 0)
    per_batch = lambda bb, i: (bb, 0, 0)
    resident = functools.partial(pl.BlockSpec, index_map=full, pipeline_mode=pl.Buffered(1))
    vec = pl.BlockSpec((1, D), full)
    ffn_specs = [pl.BlockSpec((1, 3, D), per_batch), resident(w_in.shape), resident(w_out.shape),
                 vec, vec]
    if mixer_out is None:
        body = functools.partial(_ffn_kernel, d_ff=d_ff)
        in_specs = [pl.BlockSpec((tm, D), row)] + ffn_specs
        args = (x, mod, w_in, w_out, g, b)
    else:
        o, mod0, w_o, g0, b0 = mixer_out
        body = functools.partial(_proj_ffn_kernel, d_ff=d_ff)
        in_specs = [pl.BlockSpec((tm, o.shape[1]), row), pl.BlockSpec((tm, D), row),
                    pl.BlockSpec((1, 3, D), per_batch), resident(w_o.shape), vec, vec] + ffn_specs
        args = (o, x, mod0, w_o, g0, b0, mod, w_in, w_out, g, b)
    return pl.pallas_call(
        body,
        grid=(B, nt),
        in_specs=in_specs,
        out_specs=pl.BlockSpec((tm, D), row),
        out_shape=jax.ShapeDtypeStruct((N, D), F32),
        compiler_params=_params("arbitrary", "arbitrary"),
        name="ffn",
    )(*args)


def _split3(x):
    a = x.astype(BF16)
    r = x - a.astype(F32)
    b = r.astype(BF16)
    c = (r - b.astype(F32)).astype(BF16)
    return a, b, c


def _hgrn_kernel(x_ref, mod_ref, lb_ref, w_in_ref, gn_ref, w_o_ref, g_ref, b_ref, out_ref,
                 q_s, k_s, b_s, v_s, o_s, st_s, qt_s, kt0_s, kt1_s, qe_s, kd_s, v16_s, dec_s, a_s, u_s,
                 stb_s,
                 *, layer_idx, heads, tm):
    C = HGRN_CHUNK
    SUB = HGRN_SUB
    K = HGRN_EXPAND
    HK = heads * K
    n_chunks = tm // C

    @pl.when(pl.program_id(1) == 0)
    def _():
        st_s[...] = jnp.zeros(st_s.shape, F32)

    x = x_ref[...]
    mod = mod_ref[0]
    h = _modulate(x, mod).astype(BF16)

    lb_all = lb_ref[...]
    lb_max = jnp.max(lb_all, axis=0, keepdims=True)
    lb_exp = jnp.exp(lb_all - lb_max)
    lb_soft = lb_exp / jnp.sum(lb_exp, axis=0, keepdims=True)
    lb = jnp.sum(lb_soft[:layer_idx + 1], axis=0, keepdims=True) - lb_soft[0:1]

    q_all = _silu(_dot(h, w_in_ref[:, :HK]))
    f = lb + (1.0 - lb) * jax.nn.sigmoid(_dot(h, w_in_ref[:, HK:2 * HK]))
    logf = jnp.log(f) * LOG2_E
    k_all = 1.0 - f
    v_all = _dot(h, w_in_ref[:, 2 * HK:3 * HK])

    tri = (lax.broadcasted_iota(jnp.int32, (C, C), 0)
           >= lax.broadcasted_iota(jnp.int32, (C, C), 1)).astype(BF16)
    MID = C // 2
    half_span = jnp.zeros((1, HK), F32)
    for c in range(n_chunks):
        rows = slice(c * C, (c + 1) * C)
        g1, g2, g3 = _split3(logf[rows])
        bc = _dot(tri, g1) + _dot(tri, g2) + _dot(tri, g3)
        b_s[rows, :] = bc
        half_span = jnp.maximum(half_span, jnp.maximum(bc[0:1] - bc[MID:MID + 1],
                                                       bc[MID:MID + 1] - bc[C - 1:C]))
    q_s[...] = q_all
    k_s[...] = k_all
    v_s[...] = v_all
    factorable = jnp.max(half_span) <= MAX_LOG2_SPAN
    sls = [slice(hd * K, (hd + 1) * K) for hd in range(heads)]

    @pl.when(factorable)
    def _():
        G = 2 * C
        causal = (lax.broadcasted_iota(jnp.int32, (G, G), 0)
                  >= lax.broadcasted_iota(jnp.int32, (G, G), 1))
        groups = [slice(g * G, (g + 1) * G) for g in range(tm // G)]
        zeros_ck = jnp.zeros((C, HK), BF16)
        for g, rows in enumerate(groups):
            ra = slice(g * G, g * G + C)
            rb = slice(g * G + C, (g + 1) * G)
            ba, bb = b_s[ra, :], b_s[rb, :]
            qa, qb = q_s[ra, :], q_s[rb, :]
            ka, kb = k_s[ra, :], k_s[rb, :]
            last_a = ba[C - 1:C]
            mid_a, mid_b = ba[MID:MID + 1], bb[MID:MID + 1]
            last = bb[C - 1:C] + last_a
            qt_s[ra, :] = (qa * jnp.exp2(ba - mid_a)).astype(BF16)
            qt_s[rb, :] = (qb * jnp.exp2(bb - mid_b)).astype(BF16)
            kt0_s[ra, :] = (ka * jnp.exp2(mid_a - ba)).astype(BF16)
            kt0_s[rb, :] = zeros_ck
            kt1_s[ra, :] = (ka * jnp.exp2((last_a - ba) + mid_b)).astype(BF16)
            kt1_s[rb, :] = (kb * jnp.exp2(mid_b - bb)).astype(BF16)
            qe_s[ra, :] = (qa * jnp.exp2(ba)).astype(BF16)
            qe_s[rb, :] = (qb * jnp.exp2(bb + last_a)).astype(BF16)
            kd_s[ra, :] = (ka * jnp.exp2(last - ba)).astype(BF16)
            kd_s[rb, :] = (kb * jnp.exp2(bb[C - 1:C] - bb)).astype(BF16)
            v16_s[rows, :] = v_s[rows, :].astype(BF16)
            dec_s[g:g + 1, :] = jnp.exp2(last)
        zeros_q = jnp.zeros((C, K), BF16)
        for g, rows in enumerate(groups):
            ra = slice(g * G, g * G + C)
            rb = slice(g * G + C, (g + 1) * G)
            for hd, sl in enumerate(sls):
                q_blk = jnp.concatenate(
                    [jnp.concatenate([qt_s[ra, sl], zeros_q], axis=1),
                     jnp.concatenate([zeros_q, qt_s[rb, sl]], axis=1)], axis=0)
                k_blk = jnp.concatenate([kt0_s[rows, sl], kt1_s[rows, sl]], axis=1)
                a_s[hd, rows, :] = jnp.where(causal, _dot_nt(q_blk, k_blk), 0.0).astype(BF16)
                u_s[g, hd] = _dot_tn(v16_s[rows, sl], kd_s[rows, sl])
        for hd, sl in enumerate(sls):
            st = st_s[hd]
            for g in range(len(groups)):
                stb_s[g, hd] = st.astype(BF16)
                st = st * dec_s[g:g + 1, sl] + u_s[g, hd]
            st_s[hd] = st
        for g, rows in enumerate(groups):
            for hd, sl in enumerate(sls):
                o_s[rows, sl] = (_dot(a_s[hd, rows, :], v16_s[rows, sl])
                                 + _dot_nt(qe_s[rows, sl], stb_s[g, hd]))

    grp_row = lax.broadcasted_iota(jnp.int32, (ROWS, K), 0)

    def chunk_head_direct(q, k, b, v, st):
        v16 = v.astype(BF16)
        b_last = b[C - 1:C]
        o_inter = _dot_nt((q * jnp.exp2(b)).astype(BF16), st.astype(BF16))
        k_dec = (k * jnp.exp2(b_last - b)).astype(BF16)
        st_new = st * jnp.exp2(b_last) + _dot_tn(v16, k_dec)
        outs = []
        for i in range(C // SUB):
            lo = i * SUB
            o_i = o_inter[lo:lo + SUB]
            if i > 0:
                ref = b[lo - 1:lo]
                q_t = (q[lo:lo + SUB] * jnp.exp2(b[lo:lo + SUB] - ref)).astype(BF16)
                k_t = (k[:lo] * jnp.exp2(ref - b[:lo])).astype(BF16)
                a = _dot_nt(q_t, k_t)
                o_i = o_i + _dot(a.astype(BF16), v16[:lo])
            for g in range(SUB // ROWS):
                glo = lo + g * ROWS
                bg = b[glo:glo + ROWS]
                qg = q[glo:glo + ROWS]
                o_g = o_i[g * ROWS:(g + 1) * ROWS]
                for s in range(lo, glo + ROWS):
                    e = jnp.exp2(bg - b[s:s + 1])
                    if s >= glo:
                        e = jnp.where(grp_row >= s - glo, e, 0.0)
                    w = jnp.sum(qg * e * k[s:s + 1], axis=-1, keepdims=True)
                    o_g = o_g + w * v[s:s + 1]
                outs.append(o_g)
        return jnp.concatenate(outs, axis=0), st_new

    @pl.when(jnp.logical_not(factorable))
    def _():
        def chunk(c, _):
            rows = pl.ds(pl.multiple_of(c * C, C), C)
            for hd, sl in enumerate(sls):
                o_h, st_new = chunk_head_direct(q_s[rows, sl], k_s[rows, sl], b_s[rows, sl],
                                                v_s[rows, sl], st_s[hd])
                st_s[hd] = st_new
                o_s[rows, sl] = o_h
            return 0

        lax.fori_loop(0, n_chunks, chunk, 0)

    gn = gn_ref[...]
    o = jnp.concatenate([_rms(o_s[:, hd * K:(hd + 1) * K], gn) for hd in range(heads)], axis=1)
    gate = _silu(_dot(h, w_in_ref[:, 3 * HK:]))
    _proj_ln_rows((o * gate).astype(BF16), x, mod, w_o_ref, g_ref[...], b_ref[...], out_ref)


def hgrn_layer(x, mod, lb_all, layer_idx, w_in, gn, w_o, g, b, B, S, tm):
    N, D = x.shape
    heads = D // HGRN_EXPAND
    nt = S // tm
    row = lambda bb, i: (bb * nt + i, 0)
    full = lambda bb, i: (0, 0)
    hs = pltpu.VMEM((tm, D), F32)
    hb = pltpu.VMEM((tm, D), BF16)
    group = 2 * HGRN_CHUNK
    n_groups = tm // group
    per_group_head = (n_groups, heads, HGRN_EXPAND, HGRN_EXPAND)
    return pl.pallas_call(
        functools.partial(_hgrn_kernel, layer_idx=layer_idx, heads=heads, tm=tm),
        grid=(B, nt),
        in_specs=[pl.BlockSpec((tm, D), row),
                  pl.BlockSpec((1, 3, D), lambda bb, i: (bb, 0, 0)),
                  pl.BlockSpec(lb_all.shape, full),
                  pl.BlockSpec(w_in.shape, full, pipeline_mode=pl.Buffered(1)),
                  pl.BlockSpec((1, HGRN_EXPAND), full),
                  pl.BlockSpec(w_o.shape, full, pipeline_mode=pl.Buffered(1)),
                  pl.BlockSpec((1, D), full),
                  pl.BlockSpec((1, D), full)],
        out_specs=pl.BlockSpec((tm, D), row),
        out_shape=jax.ShapeDtypeStruct((N, D), F32),
        scratch_shapes=[hs, hs, hs, hs, hs,
                        pltpu.VMEM((heads, HGRN_EXPAND, HGRN_EXPAND), F32),
                        hb, hb, hb, hb, hb, hb,
                        pltpu.VMEM((n_groups, D), F32),
                        pltpu.VMEM((heads, tm, group), BF16),
                        pltpu.VMEM(per_group_head, F32),
                        pltpu.VMEM(per_group_head, BF16)],
        compiler_params=_params("arbitrary", "arbitrary"),
        name="hgrn",
    )(x, mod, lb_all, w_in, gn, w_o, g, b)


def _mla_weights(w_in, w_qb, w_kvb):
    d = w_in.shape[0]
    rope_tail = HEAD_PAD - ROPE_LO - QK_ROPE
    w_in_p = jnp.concatenate(
        [w_in[:, :Q_LORA + KV_LORA], jnp.zeros((d, ROPE_LO), w_in.dtype),
         w_in[:, Q_LORA + KV_LORA:], jnp.zeros((d, rope_tail), w_in.dtype)], axis=1)
    qh = w_qb.reshape(Q_LORA, MLA_HEADS, QK_NOPE + QK_ROPE)
    w_qb_p = jnp.pad(qh, ((0, 0), (0, 0), (0, HEAD_PAD - QK_NOPE - QK_ROPE))).reshape(Q_LORA, -1)
    kvh = w_kvb.reshape(KV_LORA, MLA_HEADS, QK_NOPE + V_HEAD)
    w_k = jnp.pad(kvh[:, :, :QK_NOPE], ((0, 0), (0, 0), (0, HEAD_PAD - QK_NOPE))).reshape(KV_LORA, -1)
    w_v = kvh[:, :, QK_NOPE:].reshape(KV_LORA, -1)
    w_kvb_p = jnp.concatenate([w_k, w_v], axis=1)
    return w_in_p.astype(BF16), w_qb_p.astype(BF16), w_kvb_p.astype(BF16)


def kernel(x, c, positions, mla_w_in, mla_q_norm, mla_w_qb, mla_kv_norm, mla_w_kvb, mla_w_o,
           hgrn_lb, hgrn_w_in, hgrn_g_norm, hgrn_w_o, ffn_w_in, ffn_w_out,
           ada_w, ada_b, ln_g, ln_b):
    B, S, D = x.shape
    depth = ffn_w_in.shape[0]
    tm = min(S, 512)
    tq = min(S, 512)
    t_hgrn = min(S, 512)

    mods = ada_mods(c, ada_w, ada_b)
    ctab, stab = rope_tables(positions)
    xf = x.reshape(B * S, D)

    for layer in range(depth):
        j = layer // N_MIXERS
        mod = mods[2 * layer]
        g0 = ln_g[layer, 0].reshape(1, D)
        b0 = ln_b[layer, 0].reshape(1, D)
        mixer_out = None
        if layer % N_MIXERS == 0:
            w_in_p, w_qb_p, w_kvb_p = _mla_weights(mla_w_in[j], mla_w_qb[j], mla_w_kvb[j])
            q, k, v = mla_proj(xf, mod, w_in_p, mla_q_norm[j].reshape(1, -1), w_qb_p,
                               mla_kv_norm[j].reshape(1, -1), w_kvb_p, ctab, stab, B, S, tm)
            o = mla_attention(q, k, v, B, S, tq)
            mixer_out = (o, mod, mla_w_o[j].astype(BF16), g0, b0)
        else:
            xf = hgrn_layer(xf, mod, hgrn_lb, j, hgrn_w_in[j].astype(BF16),
                            hgrn_g_norm[j].reshape(1, -1), hgrn_w_o[j].astype(BF16),
                            g0, b0, B, S, t_hgrn)
        xf = ffn(xf, mods[2 * layer + 1], ffn_w_in[layer].astype(BF16),
                 ffn_w_out[layer].astype(BF16), ln_g[layer, 1].reshape(1, D),
                 ln_b[layer, 1].reshape(1, D), B, S, tm, mixer_out=mixer_out)
    return xf.reshape(B, S, D)
```

```python
import functools

import jax
import jax.numpy as jnp
from jax import lax
from jax.experimental import pallas as pl
from jax.experimental.pallas import tpu as pltpu

F32 = jnp.float32
BF16 = jnp.bfloat16

DEPTH = 4
N_MIXERS = 2

MLA_HEADS = 16
QK_NOPE = 64
QK_ROPE = 32
V_HEAD = 64
Q_LORA = 768
KV_LORA = 256
ROPE_THETA = 10000.0
HEAD_PAD = 128
ROPE_LO = QK_NOPE
ROPE_HALF = QK_ROPE // 2
ATTN_PAIRS_PER_STEP = 2

HGRN_EXPAND = 128
HGRN_CHUNK = 64
HGRN_SUB = 16
ROWS = 8
MAX_LOG2_SPAN = 100.0
LN_ROW_GROUP = 256

ALPHA = (2.0 * DEPTH) ** 0.25
LN_EPS = 1e-5
RMS_EPS = 1e-6
LOG2_E = 1.4426950408889634

VMEM_LIMIT_BYTES = 56 * 1024 * 1024


def _params(*semantics):
    return pltpu.CompilerParams(dimension_semantics=semantics,
                                vmem_limit_bytes=VMEM_LIMIT_BYTES)


def _dot(a, b):
    return jnp.dot(a, b, preferred_element_type=F32)


def _dot_nt(a, b):
    return lax.dot_general(a, b, (((1,), (1,)), ((), ())), preferred_element_type=F32)


def _dot_tn(a, b):
    return lax.dot_general(a, b, (((0,), (0,)), ((), ())), preferred_element_type=F32)


def _silu(x):
    return x * jax.nn.sigmoid(x)


def _modulate(x, mod):
    return x * (1.0 + mod[1:2]) + mod[0:1]


def _deepnorm_ln(x, y, mod, g, b):
    z = ALPHA * x + (1.0 + mod[2:3]) * y
    mu = jnp.mean(z, axis=-1, keepdims=True)
    zc = z - mu
    var = jnp.mean(zc * zc, axis=-1, keepdims=True)
    return zc * lax.rsqrt(var + LN_EPS) * g + b


def _rms(x, g):
    ms = jnp.mean(x * x, axis=-1, keepdims=True)
    return x * lax.rsqrt(ms + RMS_EPS) * g


def _proj_ln_rows(a, x, mod, w_ref, g, b, out_ref):
    rows = a.shape[0]
    step = min(rows, LN_ROW_GROUP)
    for r0 in range(0, rows, step):
        r = slice(r0, r0 + step)
        out_ref[r, :] = _deepnorm_ln(x[r], _dot(a[r], w_ref[...]), mod, g, b)


def _ada_kernel(c_ref, w_ref, b_ref, o_ref):
    sc = _silu(c_ref[...]).astype(BF16)
    o_ref[0] = _dot(sc, w_ref[0].astype(BF16)) + b_ref[0]


def ada_mods(c, ada_w, ada_b):
    B, D = c.shape
    L = ada_w.shape[0] * ada_w.shape[1]
    w = ada_w.reshape(L, D, 3 * D)
    b = ada_b.reshape(L, 1, 3 * D)
    out = pl.pallas_call(
        _ada_kernel,
        grid=(L, 3),
        in_specs=[pl.BlockSpec((B, D), lambda l, j: (0, 0)),
                  pl.BlockSpec((1, D, D), lambda l, j: (l, 0, j)),
                  pl.BlockSpec((1, 1, D), lambda l, j: (l, 0, j))],
        out_specs=pl.BlockSpec((1, B, D), lambda l, j: (l, 0, j)),
        out_shape=jax.ShapeDtypeStruct((L, B, 3 * D), F32),
        compiler_params=_params("arbitrary", "arbitrary"),
        name="ada_mods",
    )(c, w, b)
    return out.reshape(L, B, 3, D)


def _trig_kernel(pos_ref, inv_ref, cos_ref, sin_ref):
    ang = pos_ref[...].astype(F32) * inv_ref[...]
    cos_ref[...] = jnp.cos(ang)
    sin_ref[...] = jnp.sin(ang)


def rope_tables(positions):
    n = positions.size
    lanes = 128
    per_row = lanes // ROPE_HALF
    rows = n // per_row
    pos_rep = jnp.repeat(positions.reshape(-1), ROPE_HALF).reshape(rows, lanes)
    inv_freq = ROPE_THETA ** (-jnp.arange(0, QK_ROPE, 2, dtype=F32) / QK_ROPE)
    inv = jnp.tile(inv_freq, per_row).reshape(1, lanes)
    tr = min(rows, 512)
    cos, sin = pl.pallas_call(
        _trig_kernel,
        grid=(rows // tr,),
        in_specs=[pl.BlockSpec((tr, lanes), lambda i: (i, 0)),
                  pl.BlockSpec((1, lanes), lambda i: (0, 0))],
        out_specs=[pl.BlockSpec((tr, lanes), lambda i: (i, 0))] * 2,
        out_shape=[jax.ShapeDtypeStruct((rows, lanes), F32)] * 2,
        compiler_params=_params("arbitrary"),
        name="rope_trig",
    )(pos_rep, inv)
    cos = cos.reshape(n, ROPE_HALF)
    sin = sin.reshape(n, ROPE_HALF)
    tail = HEAD_PAD - ROPE_LO - QK_ROPE
    ctab = jnp.concatenate([jnp.ones((n, ROPE_LO), F32), cos, cos, jnp.ones((n, tail), F32)], axis=1)
    stab = jnp.concatenate([jnp.zeros((n, ROPE_LO), F32), sin, sin, jnp.zeros((n, tail), F32)], axis=1)
    return ctab, stab


def _rope_slot(x, ctab, stab, lane):
    other = jnp.where(lane < ROPE_LO + ROPE_HALF,
                      -pltpu.roll(x, HEAD_PAD - ROPE_HALF, 1),
                      pltpu.roll(x, ROPE_HALF, 1))
    return x * ctab + other * stab


def _mla_proj_kernel(x_ref, mod_ref, w_in_ref, qg_ref, w_qb_ref, kvg_ref, w_kvb_ref,
                     ctab_ref, stab_ref, q_ref, k_ref, v_ref):
    h = _modulate(x_ref[...], mod_ref[0]).astype(BF16)
    proj = _dot(h, w_in_ref[...])
    qn = _rms(proj[:, :Q_LORA], qg_ref[...]).astype(BF16)
    kvn = _rms(proj[:, Q_LORA:Q_LORA + KV_LORA], kvg_ref[...]).astype(BF16)
    kr = proj[:, Q_LORA + KV_LORA:]
    q = _dot(qn, w_qb_ref[...])
    kv = _dot(kvn, w_kvb_ref[...])
    ctab = ctab_ref[...]
    stab = stab_ref[...]
    lane = lax.broadcasted_iota(jnp.int32, ctab.shape, 1)
    kr = _rope_slot(kr, ctab, stab, lane)
    scale = (QK_NOPE + QK_ROPE) ** -0.5 * LOG2_E
    for hd in range(MLA_HEADS):
        sl = slice(hd * HEAD_PAD, (hd + 1) * HEAD_PAD)
        q_ref[:, sl] = (_rope_slot(q[:, sl], ctab, stab, lane) * scale).astype(BF16)
        k_ref[:, sl] = (kv[:, sl] + kr).astype(BF16)
    v_ref[...] = kv[:, MLA_HEADS * HEAD_PAD:].astype(BF16)


def mla_proj(x, mod, w_in_p, qg, w_qb_p, kvg, w_kvb_p, ctab, stab, B, S, tm):
    N, D = x.shape
    nt = S // tm
    row = lambda b, i: (b * nt + i, 0)
    full = lambda b, i: (0, 0)
    HP = MLA_HEADS * HEAD_PAD
    HV = MLA_HEADS * V_HEAD
    return pl.pallas_call(
        _mla_proj_kernel,
        grid=(B, nt),
        in_specs=[pl.BlockSpec((tm, D), row),
                  pl.BlockSpec((1, 3, D), lambda b, i: (b, 0, 0)),
                  pl.BlockSpec(w_in_p.shape, full),
                  pl.BlockSpec(qg.shape, full),
                  pl.BlockSpec(w_qb_p.shape, full),
                  pl.BlockSpec(kvg.shape, full),
                  pl.BlockSpec(w_kvb_p.shape, full),
                  pl.BlockSpec((tm, HEAD_PAD), row),
                  pl.BlockSpec((tm, HEAD_PAD), row)],
        out_specs=[pl.BlockSpec((tm, HP), row),
                   pl.BlockSpec((tm, HP), row),
                   pl.BlockSpec((tm, HV), row)],
        out_shape=[jax.ShapeDtypeStruct((N, HP), BF16),
                   jax.ShapeDtypeStruct((N, HP), BF16),
                   jax.ShapeDtypeStruct((N, HV), BF16)],
        compiler_params=_params("arbitrary", "arbitrary"),
        name="mla_proj",
    )(x, mod, w_in_p, qg, w_qb_p, kvg, w_kvb_p, ctab, stab)


def _attn_kernel(q_ref, k_ref, v_ref, o_ref, *, tq, seq_len, pairs):
    causal = (lax.broadcasted_iota(jnp.int32, (tq, tq), 0)
              >= lax.broadcasted_iota(jnp.int32, (tq, tq), 1))
    first = lax.broadcasted_iota(jnp.int32, (tq, 2 * V_HEAD), 1) < V_HEAD

    def head_rows(q0, hd):
        n = q0 + tq
        cols = slice(hd * HEAD_PAD, (hd + 1) * HEAD_PAD)
        vcols = slice((hd // 2) * 2 * V_HEAD, (hd // 2 + 1) * 2 * V_HEAD)
        q = q_ref[q0:n, cols]
        s_diag = jnp.where(causal, _dot_nt(q, k_ref[q0:n, cols]), -1e30)
        m = jnp.max(s_diag, axis=-1, keepdims=True)
        if q0 > 0:
            s_prev = _dot_nt(q, k_ref[:q0, cols])
            m = jnp.maximum(m, jnp.max(s_prev, axis=-1, keepdims=True))
        p_diag = jnp.exp2(s_diag - m)
        l = jnp.sum(p_diag, axis=-1, keepdims=True)
        o = _dot(p_diag.astype(BF16), v_ref[q0:n, vcols])
        if q0 > 0:
            p_prev = jnp.exp2(s_prev - m)
            l = l + jnp.sum(p_prev, axis=-1, keepdims=True)
            o = o + _dot(p_prev.astype(BF16), v_ref[:q0, vcols])
        return o / l

    for qi in reversed(range(seq_len // tq)):
        q0 = qi * tq
        for hp in range(pairs):
            o = jnp.where(first, head_rows(q0, 2 * hp), head_rows(q0, 2 * hp + 1))
            o_ref[q0:q0 + tq, hp * 2 * V_HEAD:(hp + 1) * 2 * V_HEAD] = o.astype(BF16)


def mla_attention(q, k, v, B, S, tq):
    N = q.shape[0]
    pairs = ATTN_PAIRS_PER_STEP
    HV = MLA_HEADS * V_HEAD
    blk = lambda b, g: (b, g)
    return pl.pallas_call(
        functools.partial(_attn_kernel, tq=tq, seq_len=S, pairs=pairs),
        grid=(B, MLA_HEADS // (2 * pairs)),
        in_specs=[pl.BlockSpec((S, pairs * 2 * HEAD_PAD), blk),
                  pl.BlockSpec((S, pairs * 2 * HEAD_PAD), blk),
                  pl.BlockSpec((S, pairs * 2 * V_HEAD), blk)],
        out_specs=pl.BlockSpec((S, pairs * 2 * V_HEAD), blk),
        out_shape=jax.ShapeDtypeStruct((N, HV), BF16),
        compiler_params=_params("arbitrary", "arbitrary"),
        name="mla_attention",
    )(q, k, v)


def _ffn_rows(x, mod, w_in_ref, w_out_ref, g, b, d_ff):
    h = _modulate(x, mod).astype(BF16)
    gate = _dot(h, w_in_ref[:, :d_ff])
    up = _dot(h, w_in_ref[:, d_ff:])
    act = (_silu(gate) * up).astype(BF16)
    return _deepnorm_ln(x, _dot(act, w_out_ref[...]), mod, g, b)


def _row_groups(rows):
    step = min(rows, LN_ROW_GROUP)
    return [slice(r0, r0 + step) for r0 in range(0, rows, step)]


def _ffn_kernel(x_ref, mod_ref, w_in_ref, w_out_ref, g_ref, b_ref, out_ref, *, d_ff):
    for r in _row_groups(x_ref.shape[0]):
        out_ref[r, :] = _ffn_rows(x_ref[r, :], mod_ref[0], w_in_ref, w_out_ref,
                                  g_ref[...], b_ref[...], d_ff)


def _proj_ffn_kernel(o_ref, x_ref, mod0_ref, w_o_ref, g0_ref, b0_ref,
                     mod_ref, w_in_ref, w_out_ref, g_ref, b_ref, out_ref, *, d_ff):
    for r in _row_groups(x_ref.shape[0]):
        x_mid = _deepnorm_ln(x_ref[r, :], _dot(o_ref[r, :], w_o_ref[...]), mod0_ref[0],
                             g0_ref[...], b0_ref[...])
        out_ref[r, :] = _ffn_rows(x_mid, mod_ref[0], w_in_ref, w_out_ref,
                                  g_ref[...], b_ref[...], d_ff)


def _resident_layer(stacked, layer):
    return pl.BlockSpec((None,) + stacked.shape[1:], lambda bb, i: (layer, 0, 0),
                        pipeline_mode=pl.Buffered(1))


def ffn(x, mod, w_in, w_out, layer, g, b, B, S, tm, mixer_out=None):
    N, D = x.shape
    d_ff = w_out.shape[1]
    nt = S // tm
    row = lambda bb, i: (bb * nt + i, 0)
    full = lambda bb, i: (0, 0)
    per_batch = lambda bb, i: (bb, 0, 0)
    vec = pl.BlockSpec((1, D), full)
    ffn_specs = [pl.BlockSpec((1, 3, D), per_batch), _resident_layer(w_in, layer),
                 _resident_layer(w_out, layer), vec, vec]
    if mixer_out is None:
        body = functools.partial(_ffn_kernel, d_ff=d_ff)
        in_specs = [pl.BlockSpec((tm, D), row)] + ffn_specs
        args = (x, mod, w_in, w_out, g, b)
    else:
        o, mod0, w_o, j, g0, b0 = mixer_out
        body = functools.partial(_proj_ffn_kernel, d_ff=d_ff)
        in_specs = [pl.BlockSpec((tm, o.shape[1]), row), pl.BlockSpec((tm, D), row),
                    pl.BlockSpec((1, 3, D), per_batch), _resident_layer(w_o, j), vec, vec] + ffn_specs
        args = (o, x, mod0, w_o, g0, b0, mod, w_in, w_out, g, b)
    return pl.pallas_call(
        body,
        grid=(B, nt),
        in_specs=in_specs,
        out_specs=pl.BlockSpec((tm, D), row),
        out_shape=jax.ShapeDtypeStruct((N, D), F32),
        compiler_params=_params("arbitrary", "arbitrary"),
        name="ffn",
    )(*args)


def _split3(x):
    a = x.astype(BF16)
    r = x - a.astype(F32)
    b = r.astype(BF16)
    c = (r - b.astype(F32)).astype(BF16)
    return a, b, c


def _hgrn_kernel(x_ref, mod_ref, lb_ref, w_in_ref, gn_ref, w_o_ref, g_ref, b_ref, out_ref,
                 q_s, k_s, b_s, v_s, o_s, st_s, qt_s, kt0_s, kt1_s, qe_s, kd_s, v16_s, dec_s, a_s, u_s,
                 stb_s,
                 *, layer_idx, heads, tm):
    C = HGRN_CHUNK
    SUB = HGRN_SUB
    K = HGRN_EXPAND
    HK = heads * K
    n_chunks = tm // C

    @pl.when(pl.program_id(1) == 0)
    def _():
        st_s[...] = jnp.zeros(st_s.shape, F32)

    x = x_ref[...]
    mod = mod_ref[0]
    h = _modulate(x, mod).astype(BF16)

    lb_all = lb_ref[...]
    lb_max = jnp.max(lb_all, axis=0, keepdims=True)
    lb_exp = jnp.exp(lb_all - lb_max)
    lb_soft = lb_exp / jnp.sum(lb_exp, axis=0, keepdims=True)
    lb = jnp.sum(lb_soft[:layer_idx + 1], axis=0, keepdims=True) - lb_soft[0:1]

    q_all = _silu(_dot(h, w_in_ref[:, :HK]))
    f = lb + (1.0 - lb) * jax.nn.sigmoid(_dot(h, w_in_ref[:, HK:2 * HK]))
    logf = jnp.log(f) * LOG2_E
    k_all = 1.0 - f
    v_all = _dot(h, w_in_ref[:, 2 * HK:3 * HK])

    tri = (lax.broadcasted_iota(jnp.int32, (C, C), 0)
           >= lax.broadcasted_iota(jnp.int32, (C, C), 1)).astype(BF16)
    MID = C // 2
    half_span = jnp.zeros((1, HK), F32)
    for c in range(n_chunks):
        rows = slice(c * C, (c + 1) * C)
        g1, g2, g3 = _split3(logf[rows])
        bc = _dot(tri, g1) + _dot(tri, g2) + _dot(tri, g3)
        b_s[rows, :] = bc
        half_span = jnp.maximum(half_span, jnp.maximum(bc[0:1] - bc[MID:MID + 1],
                                                       bc[MID:MID + 1] - bc[C - 1:C]))
    q_s[...] = q_all
    k_s[...] = k_all
    v_s[...] = v_all
    factorable = jnp.max(half_span) <= MAX_LOG2_SPAN
    sls = [slice(hd * K, (hd + 1) * K) for hd in range(heads)]

    @pl.when(factorable)
    def _():
        G = 2 * C
        causal = (lax.broadcasted_iota(jnp.int32, (G, G), 0)
                  >= lax.broadcasted_iota(jnp.int32, (G, G), 1))
        groups = [slice(g * G, (g + 1) * G) for g in range(tm // G)]
        zeros_ck = jnp.zeros((C, HK), BF16)
        for g, rows in enumerate(groups):
            ra = slice(g * G, g * G + C)
            rb = slice(g * G + C, (g + 1) * G)
            ba, bb = b_s[ra, :], b_s[rb, :]
            qa, qb = q_s[ra, :], q_s[rb, :]
            ka, kb = k_s[ra, :], k_s[rb, :]
            last_a = ba[C - 1:C]
            mid_a, mid_b = ba[MID:MID + 1], bb[MID:MID + 1]
            last = bb[C - 1:C] + last_a
            qt_s[ra, :] = (qa * jnp.exp2(ba - mid_a)).astype(BF16)
            qt_s[rb, :] = (qb * jnp.exp2(bb - mid_b)).astype(BF16)
            kt0_s[ra, :] = (ka * jnp.exp2(mid_a - ba)).astype(BF16)
            kt0_s[rb, :] = zeros_ck
            kt1_s[ra, :] = (ka * jnp.exp2((last_a - ba) + mid_b)).astype(BF16)
            kt1_s[rb, :] = (kb * jnp.exp2(mid_b - bb)).astype(BF16)
            qe_s[ra, :] = (qa * jnp.exp2(ba)).astype(BF16)
            qe_s[rb, :] = (qb * jnp.exp2(bb + last_a)).astype(BF16)
            kd_s[ra, :] = (ka * jnp.exp2(last - ba)).astype(BF16)
            kd_s[rb, :] = (kb * jnp.exp2(bb[C - 1:C] - bb)).astype(BF16)
            v16_s[rows, :] = v_s[rows, :].astype(BF16)
            dec_s[g:g + 1, :] = jnp.exp2(last)
        zeros_q = jnp.zeros((C, K), BF16)
        for g, rows in enumerate(groups):
            ra = slice(g * G, g * G + C)
            rb = slice(g * G + C, (g + 1) * G)
            for hd, sl in enumerate(sls):
                q_blk = jnp.concatenate(
                    [jnp.concatenate([qt_s[ra, sl], zeros_q], axis=1),
                     jnp.concatenate([zeros_q, qt_s[rb, sl]], axis=1)], axis=0)
                k_blk = jnp.concatenate([kt0_s[rows, sl], kt1_s[rows, sl]], axis=1)
                a_s[hd, rows, :] = jnp.where(causal, _dot_nt(q_blk, k_blk), 0.0).astype(BF16)
                u_s[g, hd] = _dot_tn(v16_s[rows, sl], kd_s[rows, sl])
        for hd, sl in enumerate(sls):
            st = st_s[hd]
            for g in range(len(groups)):
                stb_s[g, hd] = st.astype(BF16)
                st = st * dec_s[g:g + 1, sl] + u_s[g, hd]
            st_s[hd] = st
        for g, rows in enumerate(groups):
            for hd, sl in enumerate(sls):
                o_s[rows, sl] = (_dot(a_s[hd, rows, :], v16_s[rows, sl])
                                 + _dot_nt(qe_s[rows, sl], stb_s[g, hd]))

    grp_row = lax.broadcasted_iota(jnp.int32, (ROWS, K), 0)

    def chunk_head_direct(q, k, b, v, st):
        v16 = v.astype(BF16)
        b_last = b[C - 1:C]
        o_inter = _dot_nt((q * jnp.exp2(b)).astype(BF16), st.astype(BF16))
        k_dec = (k * jnp.exp2(b_last - b)).astype(BF16)
        st_new = st * jnp.exp2(b_last) + _dot_tn(v16, k_dec)
        outs = []
        for i in range(C // SUB):
            lo = i * SUB
            o_i = o_inter[lo:lo + SUB]
            if i > 0:
                ref = b[lo - 1:lo]
                q_t = (q[lo:lo + SUB] * jnp.exp2(b[lo:lo + SUB] - ref)).astype(BF16)
                k_t = (k[:lo] * jnp.exp2(ref - b[:lo])).astype(BF16)
                a = _dot_nt(q_t, k_t)
                o_i = o_i + _dot(a.astype(BF16), v16[:lo])
            for g in range(SUB // ROWS):
                glo = lo + g * ROWS
                bg = b[glo:glo + ROWS]
                qg = q[glo:glo + ROWS]
                o_g = o_i[g * ROWS:(g + 1) * ROWS]
                for s in range(lo, glo + ROWS):
                    e = jnp.exp2(bg - b[s:s + 1])
                    if s >= glo:
                        e = jnp.where(grp_row >= s - glo, e, 0.0)
                    w = jnp.sum(qg * e * k[s:s + 1], axis=-1, keepdims=True)
                    o_g = o_g + w * v[s:s + 1]
                outs.append(o_g)
        return jnp.concatenate(outs, axis=0), st_new

    @pl.when(jnp.logical_not(factorable))
    def _():
        def chunk(c, _):
            rows = pl.ds(pl.multiple_of(c * C, C), C)
            for hd, sl in enumerate(sls):
                o_h, st_new = chunk_head_direct(q_s[rows, sl], k_s[rows, sl], b_s[rows, sl],
                                                v_s[rows, sl], st_s[hd])
                st_s[hd] = st_new
                o_s[rows, sl] = o_h
            return 0

        lax.fori_loop(0, n_chunks, chunk, 0)

    gn = gn_ref[...]
    o = jnp.concatenate([_rms(o_s[:, hd * K:(hd + 1) * K], gn) for hd in range(heads)], axis=1)
    gate = _silu(_dot(h, w_in_ref[:, 3 * HK:]))
    _proj_ln_rows((o * gate).astype(BF16), x, mod, w_o_ref, g_ref[...], b_ref[...], out_ref)


def hgrn_layer(x, mod, lb_all, layer_idx, w_in, gn, w_o, g, b, B, S, tm):
    N, D = x.shape
    heads = D // HGRN_EXPAND
    nt = S // tm
    row = lambda bb, i: (bb * nt + i, 0)
    full = lambda bb, i: (0, 0)
    hs = pltpu.VMEM((tm, D), F32)
    hb = pltpu.VMEM((tm, D), BF16)
    group = 2 * HGRN_CHUNK
    n_groups = tm // group
    per_group_head = (n_groups, heads, HGRN_EXPAND, HGRN_EXPAND)
    return pl.pallas_call(
        functools.partial(_hgrn_kernel, layer_idx=layer_idx, heads=heads, tm=tm),
        grid=(B, nt),
        in_specs=[pl.BlockSpec((tm, D), row),
                  pl.BlockSpec((1, 3, D), lambda bb, i: (bb, 0, 0)),
                  pl.BlockSpec(lb_all.shape, full),
                  _resident_layer(w_in, layer_idx),
                  pl.BlockSpec((1, HGRN_EXPAND), full),
                  _resident_layer(w_o, layer_idx),
                  pl.BlockSpec((1, D), full),
                  pl.BlockSpec((1, D), full)],
        out_specs=pl.BlockSpec((tm, D), row),
        out_shape=jax.ShapeDtypeStruct((N, D), F32),
        scratch_shapes=[hs, hs, hs, hs, hs,
                        pltpu.VMEM((heads, HGRN_EXPAND, HGRN_EXPAND), F32),
                        hb, hb, hb, hb, hb, hb,
                        pltpu.VMEM((n_groups, D), F32),
                        pltpu.VMEM((heads, tm, group), BF16),
                        pltpu.VMEM(per_group_head, F32),
                        pltpu.VMEM(per_group_head, BF16)],
        compiler_params=_params("arbitrary", "arbitrary"),
        name="hgrn",
    )(x, mod, lb_all, w_in, gn, w_o, g, b)


def _mla_weights(w_in, w_qb, w_kvb):
    d = w_in.shape[0]
    rope_tail = HEAD_PAD - ROPE_LO - QK_ROPE
    w_in_p = jnp.concatenate(
        [w_in[:, :Q_LORA + KV_LORA], jnp.zeros((d, ROPE_LO), w_in.dtype),
         w_in[:, Q_LORA + KV_LORA:], jnp.zeros((d, rope_tail), w_in.dtype)], axis=1)
    qh = w_qb.reshape(Q_LORA, MLA_HEADS, QK_NOPE + QK_ROPE)
    w_qb_p = jnp.pad(qh, ((0, 0), (0, 0), (0, HEAD_PAD - QK_NOPE - QK_ROPE))).reshape(Q_LORA, -1)
    kvh = w_kvb.reshape(KV_LORA, MLA_HEADS, QK_NOPE + V_HEAD)
    w_k = jnp.pad(kvh[:, :, :QK_NOPE], ((0, 0), (0, 0), (0, HEAD_PAD - QK_NOPE))).reshape(KV_LORA, -1)
    w_v = kvh[:, :, QK_NOPE:].reshape(KV_LORA, -1)
    w_kvb_p = jnp.concatenate([w_k, w_v], axis=1)
    return w_in_p.astype(BF16), w_qb_p.astype(BF16), w_kvb_p.astype(BF16)


def kernel(x, c, positions, mla_w_in, mla_q_norm, mla_w_qb, mla_kv_norm, mla_w_kvb, mla_w_o,
           hgrn_lb, hgrn_w_in, hgrn_g_norm, hgrn_w_o, ffn_w_in, ffn_w_out,
           ada_w, ada_b, ln_g, ln_b):
    B, S, D = x.shape
    depth = ffn_w_in.shape[0]
    tm = min(S, 512)
    tq = min(S, 512)
    t_hgrn = min(S, 512)

    ffn_w_in16, ffn_w_out16 = ffn_w_in.astype(BF16), ffn_w_out.astype(BF16)
    hgrn_w_in16, hgrn_w_o16 = hgrn_w_in.astype(BF16), hgrn_w_o.astype(BF16)
    mla_w_o16 = mla_w_o.astype(BF16)
    mods = ada_mods(c, ada_w, ada_b)
    ctab, stab = rope_tables(positions)
    xf = x.reshape(B * S, D)

    for layer in range(depth):
        j = layer // N_MIXERS
        mod = mods[2 * layer]
        g0 = ln_g[layer, 0].reshape(1, D)
        b0 = ln_b[layer, 0].reshape(1, D)
        mixer_out = None
        if layer % N_MIXERS == 0:
            w_in_p, w_qb_p, w_kvb_p = _mla_weights(mla_w_in[j], mla_w_qb[j], mla_w_kvb[j])
            q, k, v = mla_proj(xf, mod, w_in_p, mla_q_norm[j].reshape(1, -1), w_qb_p,
                               mla_kv_norm[j].reshape(1, -1), w_kvb_p, ctab, stab, B, S, tm)
            o = mla_attention(q, k, v, B, S, tq)
            mixer_out = (o, mod, mla_w_o16, j, g0, b0)
        else:
            xf = hgrn_layer(xf, mod, hgrn_lb, j, hgrn_w_in16, hgrn_g_norm[j].reshape(1, -1),
                            hgrn_w_o16, g0, b0, B, S, t_hgrn)
        xf = ffn(xf, mods[2 * layer + 1], ffn_w_in16, ffn_w_out16, layer,
                 ln_g[layer, 1].reshape(1, D), ln_b[layer, 1].reshape(1, D), B, S, tm,
                 mixer_out=mixer_out)
    return xf.reshape(B, S, D)
```

```python
import functools

import jax
import jax.numpy as jnp
from jax import lax
from jax.experimental import pallas as pl
from jax.experimental.pallas import tpu as pltpu

F32 = jnp.float32
BF16 = jnp.bfloat16

DEPTH = 4
N_MIXERS = 2

MLA_HEADS = 16
QK_NOPE = 64
QK_ROPE = 32
V_HEAD = 64
Q_LORA = 768
KV_LORA = 256
ROPE_THETA = 10000.0
HEAD_PAD = 128
ROPE_LO = QK_NOPE
ROPE_HALF = QK_ROPE // 2
ATTN_PAIRS_PER_STEP = 2

HGRN_EXPAND = 128
HGRN_CHUNK = 64
HGRN_SUB = 16
ROWS = 8
MAX_LOG2_SPAN = 100.0
LN_ROW_GROUP = 256

ALPHA = (2.0 * DEPTH) ** 0.25
LN_EPS = 1e-5
RMS_EPS = 1e-6
LOG2_E = 1.4426950408889634

VMEM_LIMIT_BYTES = 56 * 1024 * 1024


def _params(*semantics):
    return pltpu.CompilerParams(dimension_semantics=semantics,
                                vmem_limit_bytes=VMEM_LIMIT_BYTES)


def _dot(a, b):
    return jnp.dot(a, b, preferred_element_type=F32)


def _dot_nt(a, b):
    return lax.dot_general(a, b, (((1,), (1,)), ((), ())), preferred_element_type=F32)


def _dot_tn(a, b):
    return lax.dot_general(a, b, (((0,), (0,)), ((), ())), preferred_element_type=F32)


def _silu(x):
    return x * jax.nn.sigmoid(x)


def _modulate(x, mod):
    return x * (1.0 + mod[1:2]) + mod[0:1]


def _deepnorm_ln(x, y, mod, g, b):
    z = ALPHA * x + (1.0 + mod[2:3]) * y
    mu = jnp.mean(z, axis=-1, keepdims=True)
    zc = z - mu
    var = jnp.mean(zc * zc, axis=-1, keepdims=True)
    return zc * lax.rsqrt(var + LN_EPS) * g + b


def _rms(x, g):
    ms = jnp.mean(x * x, axis=-1, keepdims=True)
    return x * lax.rsqrt(ms + RMS_EPS) * g


def _proj_ln_rows(a, x, mod, w_ref, g, b, out_ref):
    rows = a.shape[0]
    step = min(rows, LN_ROW_GROUP)
    for r0 in range(0, rows, step):
        r = slice(r0, r0 + step)
        out_ref[r, :] = _deepnorm_ln(x[r], _dot(a[r], w_ref[...]), mod, g, b)


def _ada_kernel(c_ref, w_ref, b_ref, o_ref):
    sc = _silu(c_ref[...]).astype(BF16)
    o_ref[0] = _dot(sc, w_ref[0].astype(BF16)) + b_ref[0]


def ada_mods(c, ada_w, ada_b):
    B, D = c.shape
    L = ada_w.shape[0] * ada_w.shape[1]
    w = ada_w.reshape(L, D, 3 * D)
    b = ada_b.reshape(L, 1, 3 * D)
    out = pl.pallas_call(
        _ada_kernel,
        grid=(L, 3),
        in_specs=[pl.BlockSpec((B, D), lambda l, j: (0, 0)),
                  pl.BlockSpec((1, D, D), lambda l, j: (l, 0, j)),
                  pl.BlockSpec((1, 1, D), lambda l, j: (l, 0, j))],
        out_specs=pl.BlockSpec((1, B, D), lambda l, j: (l, 0, j)),
        out_shape=jax.ShapeDtypeStruct((L, B, 3 * D), F32),
        compiler_params=_params("arbitrary", "arbitrary"),
        name="ada_mods",
    )(c, w, b)
    return out.reshape(L, B, 3, D)


def _trig_kernel(pos_ref, inv_ref, cos_ref, sin_ref):
    ang = pos_ref[...].astype(F32) * inv_ref[...]
    cos_ref[...] = jnp.cos(ang)
    sin_ref[...] = jnp.sin(ang)


def rope_tables(positions):
    n = positions.size
    lanes = 128
    per_row = lanes // ROPE_HALF
    rows = n // per_row
    pos_rep = jnp.repeat(positions.reshape(-1), ROPE_HALF).reshape(rows, lanes)
    inv_freq = ROPE_THETA ** (-jnp.arange(0, QK_ROPE, 2, dtype=F32) / QK_ROPE)
    inv = jnp.tile(inv_freq, per_row).reshape(1, lanes)
    tr = min(rows, 512)
    cos, sin = pl.pallas_call(
        _trig_kernel,
        grid=(rows // tr,),
        in_specs=[pl.BlockSpec((tr, lanes), lambda i: (i, 0)),
                  pl.BlockSpec((1, lanes), lambda i: (0, 0))],
        out_specs=[pl.BlockSpec((tr, lanes), lambda i: (i, 0))] * 2,
        out_shape=[jax.ShapeDtypeStruct((rows, lanes), F32)] * 2,
        compiler_params=_params("arbitrary"),
        name="rope_trig",
    )(pos_rep, inv)
    cos = cos.reshape(n, ROPE_HALF)
    sin = sin.reshape(n, ROPE_HALF)
    tail = HEAD_PAD - ROPE_LO - QK_ROPE
    ctab = jnp.concatenate([jnp.ones((n, ROPE_LO), F32), cos, cos, jnp.ones((n, tail), F32)], axis=1)
    stab = jnp.concatenate([jnp.zeros((n, ROPE_LO), F32), sin, sin, jnp.zeros((n, tail), F32)], axis=1)
    return ctab, stab


def _rope_slot(x, ctab, stab, lane):
    other = jnp.where(lane < ROPE_LO + ROPE_HALF,
                      -pltpu.roll(x, HEAD_PAD - ROPE_HALF, 1),
                      pltpu.roll(x, ROPE_HALF, 1))
    return x * ctab + other * stab


def _mla_proj_kernel(x_ref, mod_ref, w_in_ref, qg_ref, w_qb_ref, kvg_ref, w_kvb_ref,
                     ctab_ref, stab_ref, q_ref, k_ref, v_ref):
    h = _modulate(x_ref[...], mod_ref[0]).astype(BF16)
    proj = _dot(h, w_in_ref[...])
    qn = _rms(proj[:, :Q_LORA], qg_ref[...]).astype(BF16)
    kvn = _rms(proj[:, Q_LORA:Q_LORA + KV_LORA], kvg_ref[...]).astype(BF16)
    kr = proj[:, Q_LORA + KV_LORA:]
    q = _dot(qn, w_qb_ref[...])
    kv = _dot(kvn, w_kvb_ref[...])
    ctab = ctab_ref[...]
    stab = stab_ref[...]
    lane = lax.broadcasted_iota(jnp.int32, ctab.shape, 1)
    kr = _rope_slot(kr, ctab, stab, lane)
    scale = (QK_NOPE + QK_ROPE) ** -0.5 * LOG2_E
    for hd in range(MLA_HEADS):
        sl = slice(hd * HEAD_PAD, (hd + 1) * HEAD_PAD)
        q_ref[:, sl] = (_rope_slot(q[:, sl], ctab, stab, lane) * scale).astype(BF16)
        k_ref[:, sl] = (kv[:, sl] + kr).astype(BF16)
    v_ref[...] = kv[:, MLA_HEADS * HEAD_PAD:].astype(BF16)


def mla_proj(x, mod, w_in_p, qg, w_qb_p, kvg, w_kvb_p, ctab, stab, B, S, tm):
    N, D = x.shape
    nt = S // tm
    row = lambda b, i: (b * nt + i, 0)
    full = lambda b, i: (0, 0)
    HP = MLA_HEADS * HEAD_PAD
    HV = MLA_HEADS * V_HEAD
    return pl.pallas_call(
        _mla_proj_kernel,
        grid=(B, nt),
        in_specs=[pl.BlockSpec((tm, D), row),
                  pl.BlockSpec((1, 3, D), lambda b, i: (b, 0, 0)),
                  pl.BlockSpec(w_in_p.shape, full),
                  pl.BlockSpec(qg.shape, full),
                  pl.BlockSpec(w_qb_p.shape, full),
                  pl.BlockSpec(kvg.shape, full),
                  pl.BlockSpec(w_kvb_p.shape, full),
                  pl.BlockSpec((tm, HEAD_PAD), row),
                  pl.BlockSpec((tm, HEAD_PAD), row)],
        out_specs=[pl.BlockSpec((tm, HP), row),
                   pl.BlockSpec((tm, HP), row),
                   pl.BlockSpec((tm, HV), row)],
        out_shape=[jax.ShapeDtypeStruct((N, HP), BF16),
                   jax.ShapeDtypeStruct((N, HP), BF16),
                   jax.ShapeDtypeStruct((N, HV), BF16)],
        compiler_params=_params("arbitrary", "arbitrary"),
        name="mla_proj",
    )(x, mod, w_in_p, qg, w_qb_p, kvg, w_kvb_p, ctab, stab)


def _attn_kernel(q_ref, k_ref, v_ref, o_ref, *, tq, seq_len, pairs):
    causal = (lax.broadcasted_iota(jnp.int32, (tq, tq), 0)
              >= lax.broadcasted_iota(jnp.int32, (tq, tq), 1))
    first = lax.broadcasted_iota(jnp.int32, (tq, 2 * V_HEAD), 1) < V_HEAD

    def head_rows(q0, hd):
        n = q0 + tq
        cols = slice(hd * HEAD_PAD, (hd + 1) * HEAD_PAD)
        vcols = slice((hd // 2) * 2 * V_HEAD, (hd // 2 + 1) * 2 * V_HEAD)
        q = q_ref[q0:n, cols]
        s_diag = jnp.where(causal, _dot_nt(q, k_ref[q0:n, cols]), -1e30)
        m = jnp.max(s_diag, axis=-1, keepdims=True)
        if q0 > 0:
            s_prev = _dot_nt(q, k_ref[:q0, cols])
            m = jnp.maximum(m, jnp.max(s_prev, axis=-1, keepdims=True))
        p_diag = jnp.exp2(s_diag - m)
        l = jnp.sum(p_diag, axis=-1, keepdims=True)
        o = _dot(p_diag.astype(BF16), v_ref[q0:n, vcols])
        if q0 > 0:
            p_prev = jnp.exp2(s_prev - m)
            l = l + jnp.sum(p_prev, axis=-1, keepdims=True)
            o = o + _dot(p_prev.astype(BF16), v_ref[:q0, vcols])
        return o / l

    for qi in reversed(range(seq_len // tq)):
        q0 = qi * tq
        for hp in range(pairs):
            o = jnp.where(first, head_rows(q0, 2 * hp), head_rows(q0, 2 * hp + 1))
            o_ref[q0:q0 + tq, hp * 2 * V_HEAD:(hp + 1) * 2 * V_HEAD] = o.astype(BF16)


def mla_attention(q, k, v, B, S, tq):
    N = q.shape[0]
    pairs = ATTN_PAIRS_PER_STEP
    HV = MLA_HEADS * V_HEAD
    blk = lambda b, g: (b, g)
    return pl.pallas_call(
        functools.partial(_attn_kernel, tq=tq, seq_len=S, pairs=pairs),
        grid=(B, MLA_HEADS // (2 * pairs)),
        in_specs=[pl.BlockSpec((S, pairs * 2 * HEAD_PAD), blk),
                  pl.BlockSpec((S, pairs * 2 * HEAD_PAD), blk),
                  pl.BlockSpec((S, pairs * 2 * V_HEAD), blk)],
        out_specs=pl.BlockSpec((S, pairs * 2 * V_HEAD), blk),
        out_shape=jax.ShapeDtypeStruct((N, HV), BF16),
        compiler_params=_params("arbitrary", "arbitrary"),
        name="mla_attention",
    )(q, k, v)


def _ffn_rows(x, mod, w_in_ref, w_out_ref, g, b, d_ff):
    h = _modulate(x, mod).astype(BF16)
    gate = _dot(h, w_in_ref[:, :d_ff])
    up = _dot(h, w_in_ref[:, d_ff:])
    act = (_silu(gate) * up).astype(BF16)
    return _deepnorm_ln(x, _dot(act, w_out_ref[...]), mod, g, b)


def _row_groups(rows):
    step = min(rows, LN_ROW_GROUP)
    return [slice(r0, r0 + step) for r0 in range(0, rows, step)]


def _ffn_kernel(x_ref, mod_ref, w_in_ref, w_out_ref, g_ref, b_ref, out_ref, *, d_ff):
    for r in _row_groups(x_ref.shape[0]):
        out_ref[r, :] = _ffn_rows(x_ref[r, :], mod_ref[0], w_in_ref, w_out_ref,
                                  g_ref[...], b_ref[...], d_ff)


def _proj_ffn_kernel(o_ref, x_ref, mod0_ref, w_o_ref, g0_ref, b0_ref,
                     mod_ref, w_in_ref, w_out_ref, g_ref, b_ref, out_ref, *, d_ff):
    for r in _row_groups(x_ref.shape[0]):
        x_mid = _deepnorm_ln(x_ref[r, :], _dot(o_ref[r, :], w_o_ref[...]), mod0_ref[0],
                             g0_ref[...], b0_ref[...])
        out_ref[r, :] = _ffn_rows(x_mid, mod_ref[0], w_in_ref, w_out_ref,
                                  g_ref[...], b_ref[...], d_ff)


def _resident_layer(stacked, layer):
    return pl.BlockSpec((None,) + stacked.shape[1:], lambda bb, i: (layer, 0, 0),
                        pipeline_mode=pl.Buffered(1))


def ffn(x, mod, w_in, w_out, layer, g, b, B, S, tm, mixer_out=None):
    N, D = x.shape
    d_ff = w_out.shape[1]
    nt = S // tm
    row = lambda bb, i: (bb * nt + i, 0)
    full = lambda bb, i: (0, 0)
    per_batch = lambda bb, i: (bb, 0, 0)
    vec = pl.BlockSpec((1, D), full)
    ffn_specs = [pl.BlockSpec((1, 3, D), per_batch), _resident_layer(w_in, layer),
                 _resident_layer(w_out, layer), vec, vec]
    if mixer_out is None:
        body = functools.partial(_ffn_kernel, d_ff=d_ff)
        in_specs = [pl.BlockSpec((tm, D), row)] + ffn_specs
        args = (x, mod, w_in, w_out, g, b)
    else:
        o, mod0, w_o, j, g0, b0 = mixer_out
        body = functools.partial(_proj_ffn_kernel, d_ff=d_ff)
        in_specs = [pl.BlockSpec((tm, o.shape[1]), row), pl.BlockSpec((tm, D), row),
                    pl.BlockSpec((1, 3, D), per_batch), _resident_layer(w_o, j), vec, vec] + ffn_specs
        args = (o, x, mod0, w_o, g0, b0, mod, w_in, w_out, g, b)
    return pl.pallas_call(
        body,
        grid=(B, nt),
        in_specs=in_specs,
        out_specs=pl.BlockSpec((tm, D), row),
        out_shape=jax.ShapeDtypeStruct((N, D), F32),
        compiler_params=_params("arbitrary", "arbitrary"),
        name="ffn",
    )(*args)


def _split3(x):
    a = x.astype(BF16)
    r = x - a.astype(F32)
    b = r.astype(BF16)
    c = (r - b.astype(F32)).astype(BF16)
    return a, b, c


def _hgrn_kernel(x_ref, mod_ref, lb_ref, w_in_ref, gn_ref, w_o_ref, g_ref, b_ref, out_ref,
                 q_s, k_s, b_s, v_s, o_s, st_s, qt_s, kt0_s, kt1_s, qe_s, kd_s, v16_s, dec_s, a_s, u_s,
                 stb_s,
                 *, layer_idx, heads, tm):
    C = HGRN_CHUNK
    SUB = HGRN_SUB
    K = HGRN_EXPAND
    HK = heads * K
    n_chunks = tm // C

    @pl.when(pl.program_id(1) == 0)
    def _():
        st_s[...] = jnp.zeros(st_s.shape, F32)

    x = x_ref[...]
    mod = mod_ref[0]
    h = _modulate(x, mod).astype(BF16)

    lb_all = lb_ref[...]
    lb_max = jnp.max(lb_all, axis=0, keepdims=True)
    lb_exp = jnp.exp(lb_all - lb_max)
    lb_soft = lb_exp / jnp.sum(lb_exp, axis=0, keepdims=True)
    lb = jnp.sum(lb_soft[:layer_idx + 1], axis=0, keepdims=True) - lb_soft[0:1]

    q_all = _silu(_dot(h, w_in_ref[:, :HK]))
    f = lb + (1.0 - lb) * jax.nn.sigmoid(_dot(h, w_in_ref[:, HK:2 * HK]))
    logf = jnp.log(f) * LOG2_E
    k_all = 1.0 - f
    v_all = _dot(h, w_in_ref[:, 2 * HK:3 * HK])

    tri = (lax.broadcasted_iota(jnp.int32, (C, C), 0)
           >= lax.broadcasted_iota(jnp.int32, (C, C), 1)).astype(BF16)
    MID = C // 2
    half_span = jnp.zeros((1, HK), F32)
    for c in range(n_chunks):
        rows = slice(c * C, (c + 1) * C)
        g1, g2, g3 = _split3(logf[rows])
        bc = _dot(tri, g1) + _dot(tri, g2) + _dot(tri, g3)
        b_s[rows, :] = bc
        half_span = jnp.maximum(half_span, jnp.maximum(bc[0:1] - bc[MID:MID + 1],
                                                       bc[MID:MID + 1] - bc[C - 1:C]))
    q_s[...] = q_all
    k_s[...] = k_all
    v_s[...] = v_all
    factorable = jnp.max(half_span) <= MAX_LOG2_SPAN
    sls = [slice(hd * K, (hd + 1) * K) for hd in range(heads)]

    @pl.when(factorable)
    def _():
        G = 2 * C
        causal = (lax.broadcasted_iota(jnp.int32, (G, G), 0)
                  >= lax.broadcasted_iota(jnp.int32, (G, G), 1))
        groups = [slice(g * G, (g + 1) * G) for g in range(tm // G)]
        zeros_ck = jnp.zeros((C, HK), BF16)
        for g, rows in enumerate(groups):
            ra = slice(g * G, g * G + C)
            rb = slice(g * G + C, (g + 1) * G)
            ba, bb = b_s[ra, :], b_s[rb, :]
            qa, qb = q_s[ra, :], q_s[rb, :]
            ka, kb = k_s[ra, :], k_s[rb, :]
            last_a = ba[C - 1:C]
            mid_a, mid_b = ba[MID:MID + 1], bb[MID:MID + 1]
            last = bb[C - 1:C] + last_a
            qt_s[ra, :] = (qa * jnp.exp2(ba - mid_a)).astype(BF16)
            qt_s[rb, :] = (qb * jnp.exp2(bb - mid_b)).astype(BF16)
            kt0_s[ra, :] = (ka * jnp.exp2(mid_a - ba)).astype(BF16)
            kt0_s[rb, :] = zeros_ck
            kt1_s[ra, :] = (ka * jnp.exp2((last_a - ba) + mid_b)).astype(BF16)
            kt1_s[rb, :] = (kb * jnp.exp2(mid_b - bb)).astype(BF16)
            qe_s[ra, :] = (qa * jnp.exp2(ba)).astype(BF16)
            qe_s[rb, :] = (qb * jnp.exp2(bb + last_a)).astype(BF16)
            kd_s[ra, :] = (ka * jnp.exp2(last - ba)).astype(BF16)
            kd_s[rb, :] = (kb * jnp.exp2(bb[C - 1:C] - bb)).astype(BF16)
            v16_s[rows, :] = v_s[rows, :].astype(BF16)
            dec_s[g:g + 1, :] = jnp.exp2(last)
        zeros_q = jnp.zeros((C, K), BF16)
        for g, rows in enumerate(groups):
            ra = slice(g * G, g * G + C)
            rb = slice(g * G + C, (g + 1) * G)
            for hd, sl in enumerate(sls):
                q_blk = jnp.concatenate(
                    [jnp.concatenate([qt_s[ra, sl], zeros_q], axis=1),
                     jnp.concatenate([zeros_q, qt_s[rb, sl]], axis=1)], axis=0)
                k_blk = jnp.concatenate([kt0_s[rows, sl], kt1_s[rows, sl]], axis=1)
                a_s[hd, rows, :] = jnp.where(causal, _dot_nt(q_blk, k_blk), 0.0).astype(BF16)
                u_s[g, hd] = _dot_tn(v16_s[rows, sl], kd_s[rows, sl])
        for hd, sl in enumerate(sls):
            st = st_s[hd]
            for g in range(len(groups)):
                stb_s[g, hd] = st.astype(BF16)
                st = st * dec_s[g:g + 1, sl] + u_s[g, hd]
            st_s[hd] = st
        for g, rows in enumerate(groups):
            for hd, sl in enumerate(sls):
                o_s[rows, sl] = (_dot(a_s[hd, rows, :], v16_s[rows, sl])
                                 + _dot_nt(qe_s[rows, sl], stb_s[g, hd]))

    grp_row = lax.broadcasted_iota(jnp.int32, (ROWS, K), 0)

    def chunk_head_direct(q, k, b, v, st):
        v16 = v.astype(BF16)
        b_last = b[C - 1:C]
        o_inter = _dot_nt((q * jnp.exp2(b)).astype(BF16), st.astype(BF16))
        k_dec = (k * jnp.exp2(b_last - b)).astype(BF16)
        st_new = st * jnp.exp2(b_last) + _dot_tn(v16, k_dec)
        outs = []
        for i in range(C // SUB):
            lo = i * SUB
            o_i = o_inter[lo:lo + SUB]
            if i > 0:
                ref = b[lo - 1:lo]
                q_t = (q[lo:lo + SUB] * jnp.exp2(b[lo:lo + SUB] - ref)).astype(BF16)
                k_t = (k[:lo] * jnp.exp2(ref - b[:lo])).astype(BF16)
                a = _dot_nt(q_t, k_t)
                o_i = o_i + _dot(a.astype(BF16), v16[:lo])
            for g in range(SUB // ROWS):
                glo = lo + g * ROWS
                bg = b[glo:glo + ROWS]
                qg = q[glo:glo + ROWS]
                o_g = o_i[g * ROWS:(g + 1) * ROWS]
                for s in range(lo, glo + ROWS):
                    e = jnp.exp2(bg - b[s:s + 1])
                    if s >= glo:
                        e = jnp.where(grp_row >= s - glo, e, 0.0)
                    w = jnp.sum(qg * e * k[s:s + 1], axis=-1, keepdims=True)
                    o_g = o_g + w * v[s:s + 1]
                outs.append(o_g)
        return jnp.concatenate(outs, axis=0), st_new

    @pl.when(jnp.logical_not(factorable))
    def _():
        def chunk(c, _):
            rows = pl.ds(pl.multiple_of(c * C, C), C)
            for hd, sl in enumerate(sls):
                o_h, st_new = chunk_head_direct(q_s[rows, sl], k_s[rows, sl], b_s[rows, sl],
                                                v_s[rows, sl], st_s[hd])
                st_s[hd] = st_new
                o_s[rows, sl] = o_h
            return 0

        lax.fori_loop(0, n_chunks, chunk, 0)

    gn = gn_ref[...]
    o = jnp.concatenate([_rms(o_s[:, hd * K:(hd + 1) * K], gn) for hd in range(heads)], axis=1)
    gate = _silu(_dot(h, w_in_ref[:, 3 * HK:]))
    _proj_ln_rows((o * gate).astype(BF16), x, mod, w_o_ref, g_ref[...], b_ref[...], out_ref)


def hgrn_layer(x, mod, lb_all, layer_idx, w_in, gn, w_o, g, b, B, S, tm):
    N, D = x.shape
    heads = D // HGRN_EXPAND
    nt = S // tm
    row = lambda bb, i: (bb * nt + i, 0)
    full = lambda bb, i: (0, 0)
    hs = pltpu.VMEM((tm, D), F32)
    hb = pltpu.VMEM((tm, D), BF16)
    group = 2 * HGRN_CHUNK
    n_groups = tm // group
    per_group_head = (n_groups, heads, HGRN_EXPAND, HGRN_EXPAND)
    return pl.pallas_call(
        functools.partial(_hgrn_kernel, layer_idx=layer_idx, heads=heads, tm=tm),
        grid=(B, nt),
        in_specs=[pl.BlockSpec((tm, D), row),
                  pl.BlockSpec((1, 3, D), lambda bb, i: (bb, 0, 0)),
                  pl.BlockSpec(lb_all.shape, full),
                  _resident_layer(w_in, layer_idx),
                  pl.BlockSpec((1, HGRN_EXPAND), full),
                  _resident_layer(w_o, layer_idx),
                  pl.BlockSpec((1, D), full),
                  pl.BlockSpec((1, D), full)],
        out_specs=pl.BlockSpec((tm, D), row),
        out_shape=jax.ShapeDtypeStruct((N, D), F32),
        scratch_shapes=[hs, hs, hs, hs, hs,
                        pltpu.VMEM((heads, HGRN_EXPAND, HGRN_EXPAND), F32),
                        hb, hb, hb, hb, hb, hb,
                        pltpu.VMEM((n_groups, D), F32),
                        pltpu.VMEM((heads, tm, group), BF16),
                        pltpu.VMEM(per_group_head, F32),
                        pltpu.VMEM(per_group_head, BF16)],
        compiler_params=_params("arbitrary", "arbitrary"),
        name="hgrn",
    )(x, mod, lb_all, w_in, gn, w_o, g, b)


def _mla_weights(w_in, w_qb, w_kvb):
    d = w_in.shape[0]
    rope_tail = HEAD_PAD - ROPE_LO - QK_ROPE
    w_in_p = jnp.concatenate(
        [w_in[:, :Q_LORA + KV_LORA], jnp.zeros((d, ROPE_LO), w_in.dtype),
         w_in[:, Q_LORA + KV_LORA:], jnp.zeros((d, rope_tail), w_in.dtype)], axis=1)
    qh = w_qb.reshape(Q_LORA, MLA_HEADS, QK_NOPE + QK_ROPE)
    w_qb_p = jnp.pad(qh, ((0, 0), (0, 0), (0, HEAD_PAD - QK_NOPE - QK_ROPE))).reshape(Q_LORA, -1)
    kvh = w_kvb.reshape(KV_LORA, MLA_HEADS, QK_NOPE + V_HEAD)
    w_k = jnp.pad(kvh[:, :, :QK_NOPE], ((0, 0), (0, 0), (0, HEAD_PAD - QK_NOPE))).reshape(KV_LORA, -1)
    w_v = kvh[:, :, QK_NOPE:].reshape(KV_LORA, -1)
    w_kvb_p = jnp.concatenate([w_k, w_v], axis=1)
    return w_in_p.astype(BF16), w_qb_p.astype(BF16), w_kvb_p.astype(BF16)


def kernel(x, c, positions, mla_w_in, mla_q_norm, mla_w_qb, mla_kv_norm, mla_w_kvb, mla_w_o,
           hgrn_lb, hgrn_w_in, hgrn_g_norm, hgrn_w_o, ffn_w_in, ffn_w_out,
           ada_w, ada_b, ln_g, ln_b):
    B, S, D = x.shape
    depth = ffn_w_in.shape[0]
    tm = min(S, 512)
    tq = min(S, 512)
    t_hgrn = min(S, 512)
    t_ffn = min(S, 1024)

    ffn_w_in16, ffn_w_out16 = ffn_w_in.astype(BF16), ffn_w_out.astype(BF16)
    hgrn_w_in16, hgrn_w_o16 = hgrn_w_in.astype(BF16), hgrn_w_o.astype(BF16)
    mla_w_o16 = mla_w_o.astype(BF16)
    mods = ada_mods(c, ada_w, ada_b)
    ctab, stab = rope_tables(positions)
    xf = x.reshape(B * S, D)

    for layer in range(depth):
        j = layer // N_MIXERS
        mod = mods[2 * layer]
        g0 = ln_g[layer, 0].reshape(1, D)
        b0 = ln_b[layer, 0].reshape(1, D)
        mixer_out = None
        if layer % N_MIXERS == 0:
            w_in_p, w_qb_p, w_kvb_p = _mla_weights(mla_w_in[j], mla_w_qb[j], mla_w_kvb[j])
            q, k, v = mla_proj(xf, mod, w_in_p, mla_q_norm[j].reshape(1, -1), w_qb_p,
                               mla_kv_norm[j].reshape(1, -1), w_kvb_p, ctab, stab, B, S, tm)
            o = mla_attention(q, k, v, B, S, tq)
            mixer_out = (o, mod, mla_w_o16, j, g0, b0)
        else:
            xf = hgrn_layer(xf, mod, hgrn_lb, j, hgrn_w_in16, hgrn_g_norm[j].reshape(1, -1),
                            hgrn_w_o16, g0, b0, B, S, t_hgrn)
        xf = ffn(xf, mods[2 * layer + 1], ffn_w_in16, ffn_w_out16, layer,
                 ln_g[layer, 1].reshape(1, D), ln_b[layer, 1].reshape(1, D), B, S, t_ffn,
                 mixer_out=mixer_out)
    return xf.reshape(B, S, D)
```
